```python
import jax, jax.numpy as jnp
from jax import lax
import numpy as np

D_MODEL = 2048
BATCH = 2
SEQ = 8192
DEPTH = 2

N_META = 16
BLOCK = 128
META_PAD = BLOCK - N_META
SB_HEADS = 8
SB_HEAD_DIM = 128
SB_W = SB_HEADS * SB_HEAD_DIM
MLA_HEADS = 8
Q_LORA = 512
KV_LORA = 256
NOPE_DIM = 128
ROPE_DIM = 64
V_DIM = 128
MLA_W = MLA_HEADS * V_DIM
ROPE_THETA = 10000.0
D_FF = 5632
CONV_W = 3
EPS = 1e-6
IN_WIDTHS = (SB_W, SB_W, SB_W, Q_LORA, KV_LORA, ROPE_DIM, D_MODEL, D_MODEL)
D_IN = SB_W * 3 + Q_LORA + KV_LORA + ROPE_DIM + 2 * D_MODEL

kernel_name = "hybrid_stickbreak_mla_convglu"


def _rmsnorm(x, g):
    xf = x.astype(jnp.float32)
    y = xf * lax.rsqrt(jnp.mean(xf * xf, axis=-1, keepdims=True) + EPS)
    return (y * g.astype(jnp.float32)).astype(x.dtype)


def _split(t, widths):
    pieces, off = [], 0
    for w in widths:
        pieces.append(t[..., off:off + w])
        off += w
    return pieces


def _heads(t, n_heads):
    b, l, _ = t.shape
    return t.reshape(b, l, n_heads, -1).transpose(0, 2, 1, 3)


def _merge_heads(t):
    b, h, l, d = t.shape
    return t.transpose(0, 2, 1, 3).reshape(b, l, h * d)


def _rope(t, pos):
    half = t.shape[-1] // 2
    freqs = ROPE_THETA ** (-jnp.arange(half, dtype=jnp.float32) / half)
    ang = pos.astype(jnp.float32)[:, None] * freqs[None, :]
    cos, sin = jnp.cos(ang), jnp.sin(ang)
    tf = t.astype(jnp.float32)
    t1, t2 = tf[..., :half], tf[..., half:]
    return jnp.concatenate([t1 * cos - t2 * sin, t1 * sin + t2 * cos], axis=-1).astype(t.dtype)


def _pad_seq(t, axis):
    widths = [(0, 0)] * t.ndim
    widths[axis] = (META_PAD, 0)
    return jnp.pad(t, widths)


def _to_blocks(t):
    b, h, lp, d = t.shape
    return t.reshape(b, h, lp // BLOCK, BLOCK, d).transpose(2, 0, 1, 3, 4)


def _from_blocks(t):
    nb, b, h, blk, d = t.shape
    return t.transpose(1, 2, 0, 3, 4).reshape(b, h, nb * blk, d)


def _stick_breaking_attn(q, k, v):
    qp, kp, vp = _pad_seq(q, 2), _pad_seq(k, 2), _pad_seq(v, 2)
    lp = qp.shape[2]
    k_idx = jnp.arange(lp)
    kf = kp.astype(jnp.float32)
    scale = SB_HEAD_DIM ** -0.5

    def one_block(args):
        q_blk, blk = args
        q_idx = blk * BLOCK + jnp.arange(BLOCK)
        logits = jnp.einsum('bhqd,bhkd->bhqk', q_blk.astype(jnp.float32), kf) * scale
        mask = (k_idx[None, :] < q_idx[:, None]) & (k_idx[None, :] >= META_PAD)
        log_1m_beta = jnp.where(mask, jax.nn.log_sigmoid(-logits), 0.0)
        later = lax.cumsum(log_1m_beta, axis=3, reverse=True) - log_1m_beta
        w = jnp.where(mask, jnp.exp(jax.nn.log_sigmoid(logits) + later), 0.0)
        return jnp.einsum('bhqk,bhkd->bhqd', w.astype(vp.dtype), vp)

    out = lax.map(one_block, (_to_blocks(qp), jnp.arange(lp // BLOCK)))
    return _from_blocks(out)[:, :, META_PAD:]


def _mla_attn(q_nope, q_rope, k_nope, k_rope, v):
    qn, qr = _pad_seq(q_nope, 2), _pad_seq(q_rope, 2)
    knf = _pad_seq(k_nope, 2).astype(jnp.float32)
    krf = _pad_seq(k_rope, 1).astype(jnp.float32)
    vp = _pad_seq(v, 2)
    lp = qn.shape[2]
    k_idx = jnp.arange(lp)
    scale = (NOPE_DIM + ROPE_DIM) ** -0.5

    def one_block(args):
        qn_blk, qr_blk, blk = args
        q_idx = blk * BLOCK + jnp.arange(BLOCK)
        s = (jnp.einsum('bhqd,bhkd->bhqk', qn_blk.astype(jnp.float32), knf)
             + jnp.einsum('bhqr,bkr->bhqk', qr_blk.astype(jnp.float32), krf)) * scale
        mask = (k_idx[None, :] <= q_idx[:, None]) & (k_idx[None, :] >= META_PAD)
        p = jax.nn.softmax(jnp.where(mask, s, -1e30), axis=-1)
        return jnp.einsum('bhqk,bhkd->bhqd', p.astype(vp.dtype), vp)

    out = lax.map(one_block, (_to_blocks(qn), _to_blocks(qr), jnp.arange(lp // BLOCK)))
    return _from_blocks(out)[:, :, META_PAD:]


def _causal_dwconv(u, w, b):
    l = u.shape[1]
    up = jnp.pad(u, ((0, 0), (CONV_W - 1, 0), (0, 0)))
    out = b
    for i in range(CONV_W):
        out = out + up[:, i:i + l] * w[i]
    return out


def _mixer(x, norm_mix, w_in, q_norm, w_uq, kv_norm, w_ukv, w_sb_out, w_mla_out, w_o):
    b, l, _ = x.shape
    h = _rmsnorm(x, norm_mix)
    proj = h @ w_in
    q_sb, k_sb, v_sb, c_q, c_kv, k_rope, g_sb, g_mla = _split(proj, IN_WIDTHS)
    pos = jnp.arange(l)
    o_sb = _stick_breaking_attn(_heads(q_sb, SB_HEADS), _heads(k_sb, SB_HEADS), _heads(v_sb, SB_HEADS))
    branch_sb = _merge_heads(o_sb) @ w_sb_out
    q = (_rmsnorm(c_q, q_norm) @ w_uq).reshape(b, l, MLA_HEADS, NOPE_DIM + ROPE_DIM).transpose(0, 2, 1, 3)
    q_nope, q_rope = q[..., :NOPE_DIM], _rope(q[..., NOPE_DIM:], pos)
    kv = (_rmsnorm(c_kv, kv_norm) @ w_ukv).reshape(b, l, MLA_HEADS, NOPE_DIM + V_DIM).transpose(0, 2, 1, 3)
    k_nope, v = kv[..., :NOPE_DIM], kv[..., NOPE_DIM:]
    o_mla = _mla_attn(q_nope, q_rope, k_nope, _rope(k_rope, pos), v)
    branch_mla = _merge_heads(o_mla) @ w_mla_out
    merged = jax.nn.sigmoid(g_sb) * branch_sb + jax.nn.sigmoid(g_mla) * branch_mla
    return merged @ w_o


def _conv_ffn(x, norm_ffn, w_up, conv_w, conv_b, w_down):
    h = _rmsnorm(x, norm_ffn)
    u = _causal_dwconv(h @ w_up, conv_w, conv_b)
    a, g = u[..., :D_FF], u[..., D_FF:]
    return (jax.nn.silu(a) * g) @ w_down


def setup_inputs(seed: int = 0) -> dict:
    key = jax.random.key(seed)
    ks = jax.random.split(key, 20)
    f32 = jnp.float32

    def nrm(k, shape, scale):
        return jax.random.normal(k, shape, f32) * scale

    def gain(k, shape):
        return 1.0 + 0.02 * jax.random.normal(k, shape, f32)

    return {
        "x": nrm(ks[0], (BATCH, SEQ, D_MODEL), 1.0),
        "meta_tokens": nrm(ks[1], (N_META, D_MODEL), 1.0),
        "norm_mix": gain(ks[2], (DEPTH, D_MODEL)),
        "w_in": nrm(ks[3], (DEPTH, D_MODEL, D_IN), D_MODEL ** -0.5),
        "q_norm": gain(ks[4], (DEPTH, Q_LORA)),
        "w_uq": nrm(ks[5], (DEPTH, Q_LORA, MLA_HEADS * (NOPE_DIM + ROPE_DIM)), Q_LORA ** -0.5),
        "kv_norm": gain(ks[6], (DEPTH, KV_LORA)),
        "w_ukv": nrm(ks[7], (DEPTH, KV_LORA, MLA_HEADS * (NOPE_DIM + V_DIM)), KV_LORA ** -0.5),
        "w_sb_out": nrm(ks[8], (DEPTH, SB_W, D_MODEL), SB_W ** -0.5),
        "w_mla_out": nrm(ks[9], (DEPTH, MLA_W, D_MODEL), MLA_W ** -0.5),
        "w_o": nrm(ks[10], (DEPTH, D_MODEL, D_MODEL), D_MODEL ** -0.5),
        "norm_ffn": gain(ks[11], (DEPTH, D_MODEL)),
        "w_up": nrm(ks[12], (DEPTH, D_MODEL, 2 * D_FF), D_MODEL ** -0.5),
        "conv_w": nrm(ks[13], (DEPTH, CONV_W, 2 * D_FF), CONV_W ** -0.5),
        "conv_b": nrm(ks[14], (DEPTH, 2 * D_FF), 0.02),
        "w_down": nrm(ks[15], (DEPTH, D_FF, D_MODEL), D_FF ** -0.5),
        "final_norm": gain(ks[16], (D_MODEL,)),
    }


def reference(x, meta_tokens, norm_mix, w_in, q_norm, w_uq, kv_norm, w_ukv, w_sb_out, w_mla_out,
              w_o, norm_ffn, w_up, conv_w, conv_b, w_down, final_norm):
    b = x.shape[0]
    meta = jnp.broadcast_to(meta_tokens[None].astype(x.dtype), (b, N_META, x.shape[-1]))
    h = jnp.concatenate([meta, x], axis=1)
    for i in range(DEPTH):
        h = h + _mixer(h, norm_mix[i], w_in[i], q_norm[i], w_uq[i], kv_norm[i], w_ukv[i],
                       w_sb_out[i], w_mla_out[i], w_o[i])
        h = h + _conv_ffn(h, norm_ffn[i], w_up[i], conv_w[i], conv_b[i], w_down[i])
    return _rmsnorm(h[:, N_META:], final_norm)
```

```python
import functools

import jax
import jax.numpy as jnp
from jax import lax
from jax.experimental import pallas as pl
from jax.experimental.pallas import tpu as pltpu

F32 = jnp.float32
BF16 = jnp.bfloat16

D_MODEL = 2048
N_META = 16
SB_HEADS = 8
SB_HEAD_DIM = 128
SB_W = SB_HEADS * SB_HEAD_DIM
MLA_HEADS = 8
Q_LORA = 512
KV_LORA = 256
NOPE_DIM = 128
ROPE_DIM = 64
V_DIM = 128
MLA_W = MLA_HEADS * V_DIM
MLA_QK_SLOT = 256
ROPE_THETA = 10000.0
D_FF = 5632
CONV_W = 3
EPS = 1e-6
LAT_W = Q_LORA + KV_LORA + 2 * ROPE_DIM

TILE = 256
VMEM_LIMIT = 56 * 1024 * 1024
SB_DEAD_LOG = -104.0


def _cparams(n_axes):
    return pltpu.CompilerParams(dimension_semantics=("arbitrary",) * n_axes,
                                vmem_limit_bytes=VMEM_LIMIT)


def _rms(x, g):
    return x * lax.rsqrt(jnp.mean(x * x, axis=-1, keepdims=True) + EPS) * g


def _dot(a, b):
    return jnp.dot(a, b, preferred_element_type=F32)


def _dot_nt(a, b):
    return lax.dot_general(a, b, (((1,), (1,)), ((), ())), preferred_element_type=F32)


def _rmsnorm_kernel(x_ref, g_ref, o_ref):
    o_ref[...] = _rms(x_ref[...], g_ref[...]).astype(o_ref.dtype)


def _rmsnorm(x, g, tm):
    m, d = x.shape
    return pl.pallas_call(
        _rmsnorm_kernel,
        grid=(m // tm,),
        in_specs=[pl.BlockSpec((tm, d), lambda i: (i, 0)), pl.BlockSpec((1, d), lambda i: (0, 0))],
        out_specs=pl.BlockSpec((tm, d), lambda i: (i, 0)),
        out_shape=jax.ShapeDtypeStruct((m, d), BF16),
        compiler_params=_cparams(1),
        name="rmsnorm",
    )(x, g.reshape(1, d))


def _matmul_kernel(x_ref, w_ref, o_ref, *, act):
    y = _dot(x_ref[...], w_ref[...])
    if act == "sigmoid":
        y = jax.nn.sigmoid(y)
    o_ref[...] = y.astype(o_ref.dtype)


def _matmul(x, w, out_dtype, tm, tn, act=None, name="matmul"):
    m, k = x.shape
    n = w.shape[1]
    return pl.pallas_call(
        functools.partial(_matmul_kernel, act=act),
        grid=(n // tn, m // tm),
        in_specs=[pl.BlockSpec((tm, k), lambda j, i: (i, 0)), pl.BlockSpec((k, tn), lambda j, i: (0, j))],
        out_specs=pl.BlockSpec((tm, tn), lambda j, i: (i, j)),
        out_shape=jax.ShapeDtypeStruct((m, n), out_dtype),
        compiler_params=_cparams(2),
        name=name,
    )(x, w)


def _matmul_t_kernel(x_ref, wt_ref, o_ref):
    for r in range(o_ref.shape[0]):
        o_ref[r] = _dot_nt(wt_ref[...], x_ref[r * TILE:(r + 1) * TILE, :]).astype(o_ref.dtype)


def _matmul_t(x, wt, tm, tn, name="matmul_t"):
    m, k = x.shape
    n = wt.shape[0]
    return pl.pallas_call(
        _matmul_t_kernel,
        grid=(n // tn, m // tm),
        in_specs=[pl.BlockSpec((tm, k), lambda j, i: (i, 0)), pl.BlockSpec((tn, k), lambda j, i: (j, 0))],
        out_specs=pl.BlockSpec((tm // TILE, tn, TILE), lambda j, i: (i, j, 0)),
        out_shape=jax.ShapeDtypeStruct((m // TILE, n, TILE), BF16),
        compiler_params=_cparams(2),
        name=name,
    )(x, wt)


def _mla_prep_kernel(lat_ref, gq_ref, gkv_ref, wqt_ref, wkn_ref, wvt_ref, t1_ref, t2_ref, cct_ref, sst_ref,
                     qt_ref, k_ref, vt_ref):
    lat = lat_ref[...]
    cq = _rms(lat[:, :Q_LORA], gq_ref[...]).astype(BF16)
    ckv = _rms(lat[:, Q_LORA:Q_LORA + KV_LORA], gkv_ref[...]).astype(BF16)
    kr = lat[:, Q_LORA + KV_LORA:]

    qt = _dot_nt(wqt_ref[...], cq)
    cct, sst = cct_ref[...], sst_ref[...]
    for h in range(MLA_HEADS):
        base = h * MLA_QK_SLOT
        r0, r1, r2 = base + NOPE_DIM, base + NOPE_DIM + ROPE_DIM, base + MLA_QK_SLOT
        qt_ref[base:r0, :] = qt[base:r0].astype(BF16)
        qt_ref[r0:r1, :] = (qt[r0:r1] * cct + qt[r1:r2] * sst).astype(BF16)
        qt_ref[r1:r2, :] = jnp.zeros((ROPE_DIM, TILE), BF16)

    kn = _dot(ckv, wkn_ref[...])
    k_rot = (kr * t1_ref[...] + pltpu.roll(kr, ROPE_DIM, axis=1) * t2_ref[...]).astype(BF16)
    for h in range(MLA_HEADS):
        base = h * MLA_QK_SLOT
        k_ref[:, base:base + NOPE_DIM] = kn[:, h * NOPE_DIM:(h + 1) * NOPE_DIM].astype(BF16)
        k_ref[:, base + NOPE_DIM:base + MLA_QK_SLOT] = k_rot

    vt_ref[...] = _dot_nt(wvt_ref[...], ckv).astype(BF16)


def _mla_prep(lat, gq, gkv, wqt, wkn, wvt, t1, t2, cct, sst, nblk):
    m = lat.shape[0]
    nt = m // TILE
    hq = MLA_HEADS * MLA_QK_SLOT
    const = lambda i: (0, 0)
    return pl.pallas_call(
        _mla_prep_kernel,
        grid=(nt,),
        in_specs=[
            pl.BlockSpec((TILE, LAT_W), lambda i: (i, 0)),
            pl.BlockSpec((1, Q_LORA), const),
            pl.BlockSpec((1, KV_LORA), const),
            pl.BlockSpec((hq, Q_LORA), const),
            pl.BlockSpec((KV_LORA, MLA_HEADS * NOPE_DIM), const),
            pl.BlockSpec((MLA_W, KV_LORA), const),
            pl.BlockSpec((TILE, 2 * ROPE_DIM), lambda i: (i % nblk, 0)),
            pl.BlockSpec((TILE, 2 * ROPE_DIM), lambda i: (i % nblk, 0)),
            pl.BlockSpec((ROPE_DIM, TILE), lambda i: (0, i % nblk)),
            pl.BlockSpec((ROPE_DIM, TILE), lambda i: (0, i % nblk)),
        ],
        out_specs=[
            pl.BlockSpec((None, hq, TILE), lambda i: (i, 0, 0)),
            pl.BlockSpec((TILE, hq), lambda i: (i, 0)),
            pl.BlockSpec((None, MLA_W, TILE), lambda i: (i, 0, 0)),
        ],
        out_shape=[
            jax.ShapeDtypeStruct((nt, hq, TILE), BF16),
            jax.ShapeDtypeStruct((m, hq), BF16),
            jax.ShapeDtypeStruct((nt, MLA_W, TILE), BF16),
        ],
        compiler_params=_cparams(1),
        name="mla_prep",
    )(lat, gq.reshape(1, -1), gkv.reshape(1, -1), wqt, wkn, wvt, t1, t2, cct, sst)


def _mla_attn_kernel(qt_ref, k_ref, vt_ref, o_ref):
    qi = pl.program_id(2)
    qt = qt_ref[...]
    scale = (NOPE_DIM + ROPE_DIM) ** -0.5

    def tile(j, carry, masked):
        m, l, acc = carry
        s = _dot(k_ref[j], qt) * scale
        if masked:
            key = lax.broadcasted_iota(jnp.int32, (TILE, TILE), 0)
            qry = lax.broadcasted_iota(jnp.int32, (TILE, TILE), 1)
            s = jnp.where(key <= qry, s, -1e30)
        m_new = jnp.maximum(m, jnp.max(s, axis=0, keepdims=True))
        alpha = jnp.exp(m - m_new)
        p = jnp.exp(s - m_new)
        l = alpha * l + jnp.sum(p, axis=0, keepdims=True)
        acc = alpha * acc + _dot(vt_ref[j], p.astype(BF16))
        return m_new, l, acc

    init = (jnp.full((1, TILE), -1e30, F32), jnp.zeros((1, TILE), F32), jnp.zeros((V_DIM, TILE), F32))
    carry = lax.fori_loop(0, qi, lambda j, c: tile(j, c, False), init)
    _, l, acc = tile(qi, carry, True)
    o_ref[...] = (acc / l).T.astype(o_ref.dtype)


def _mla_attn(qt, k_cat, vt, batch, nblk):
    m = k_cat.shape[0]
    k3 = k_cat.reshape(batch * nblk, TILE, MLA_HEADS * MLA_QK_SLOT)
    return pl.pallas_call(
        _mla_attn_kernel,
        grid=(batch, MLA_HEADS, nblk),
        in_specs=[
            pl.BlockSpec((None, MLA_QK_SLOT, TILE), lambda b, h, q: (b * nblk + q, h, 0)),
            pl.BlockSpec((nblk, TILE, MLA_QK_SLOT), lambda b, h, q: (b, 0, h)),
            pl.BlockSpec((nblk, V_DIM, TILE), lambda b, h, q: (b, h, 0)),
        ],
        out_specs=pl.BlockSpec((TILE, V_DIM), lambda b, h, q: (b * nblk + q, h)),
        out_shape=jax.ShapeDtypeStruct((m, MLA_W), BF16),
        compiler_params=_cparams(3),
        name="mla_attn",
    )(qt, k3, vt)


def _sb_attn_kernel(qt_ref, k_ref, vt_ref, o_ref):
    qi = pl.program_id(2)
    qt = qt_ref[...]
    scale = SB_HEAD_DIM ** -0.5
    key = lax.broadcasted_iota(jnp.int32, (TILE, TILE), 0)
    col = lax.broadcasted_iota(jnp.int32, (TILE, TILE), 1)
    later_keys = jnp.where(col > key, 1.0, 0.0).astype(BF16)

    def tile(j, carry, acc, masked):
        z = _dot(k_ref[j], qt) * scale
        soft = jnp.log1p(jnp.exp(-jnp.abs(z)))
        log_beta = jnp.minimum(z, 0.0) - soft
        log_1m_beta = log_beta - z
        if masked:
            valid = key < col
            log_1m_beta = jnp.where(valid, log_1m_beta, 0.0)
        hi = log_1m_beta.astype(BF16)
        lo = (log_1m_beta - hi.astype(F32)).astype(BF16)
        later = _dot(later_keys, hi) + _dot(later_keys, lo)
        w = jnp.exp(log_beta + later + carry)
        if masked:
            w = jnp.where(valid, w, 0.0)
        acc = acc + _dot(vt_ref[j], w.astype(BF16))
        carry = carry + jnp.sum(log_1m_beta, axis=0, keepdims=True)
        return carry, acc

    carry, acc = tile(qi, jnp.zeros((1, TILE), F32), jnp.zeros((SB_HEAD_DIM, TILE), F32), True)

    def cond(state):
        j, carry, _ = state
        return jnp.logical_and(j >= 0, jnp.max(carry) > SB_DEAD_LOG)

    def body(state):
        j, carry, acc = state
        carry, acc = tile(j, carry, acc, False)
        return j - 1, carry, acc

    _, _, acc = lax.while_loop(cond, body, (qi - 1, carry, acc))
    o_ref[...] = acc.T.astype(o_ref.dtype)


def _sb_attn(qt, k, vt, batch, nblk):
    m = k.shape[0]
    k3 = k.reshape(batch * nblk, TILE, SB_W)
    return pl.pallas_call(
        _sb_attn_kernel,
        grid=(batch, SB_HEADS, nblk),
        in_specs=[
            pl.BlockSpec((None, SB_HEAD_DIM, TILE), lambda b, h, q: (b * nblk + q, h, 0)),
            pl.BlockSpec((nblk, TILE, SB_HEAD_DIM), lambda b, h, q: (b, 0, h)),
            pl.BlockSpec((nblk, SB_HEAD_DIM, TILE), lambda b, h, q: (b, h, 0)),
        ],
        out_specs=pl.BlockSpec((TILE, SB_HEAD_DIM), lambda b, h, q: (b * nblk + q, h)),
        out_shape=jax.ShapeDtypeStruct((m, SB_W), BF16),
        compiler_params=_cparams(3),
        name="sb_attn",
    )(qt, k3, vt)


def _merge_kernel(osb_ref, omla_ref, gate_ref, h_ref, wsb_ref, wmla_ref, wo_ref, g_ref, h_out_ref, hn_out_ref):
    a = _dot(osb_ref[...], wsb_ref[...])
    b = _dot(omla_ref[...], wmla_ref[...])
    merged = gate_ref[:, :D_MODEL].astype(F32) * a + gate_ref[:, D_MODEL:].astype(F32) * b
    y = h_ref[...] + _dot(merged.astype(BF16), wo_ref[...])
    h_out_ref[...] = y
    hn_out_ref[...] = _rms(y, g_ref[...]).astype(hn_out_ref.dtype)


def _merge(o_sb, o_mla, gates, h, wsb, wmla, wo, g, tm):
    m = h.shape[0]
    row = lambda i: (i, 0)
    const = lambda i: (0, 0)
    once = pl.Buffered(1)
    return pl.pallas_call(
        _merge_kernel,
        grid=(m // tm,),
        in_specs=[
            pl.BlockSpec((tm, SB_W), row),
            pl.BlockSpec((tm, MLA_W), row),
            pl.BlockSpec((tm, 2 * D_MODEL), row),
            pl.BlockSpec((tm, D_MODEL), row),
            pl.BlockSpec((SB_W, D_MODEL), const, pipeline_mode=once),
            pl.BlockSpec((MLA_W, D_MODEL), const, pipeline_mode=once),
            pl.BlockSpec((D_MODEL, D_MODEL), const, pipeline_mode=once),
            pl.BlockSpec((1, D_MODEL), const),
        ],
        out_specs=[pl.BlockSpec((tm, D_MODEL), row), pl.BlockSpec((tm, D_MODEL), row)],
        out_shape=[jax.ShapeDtypeStruct((m, D_MODEL), F32), jax.ShapeDtypeStruct((m, D_MODEL), BF16)],
        compiler_params=_cparams(1),
        name="merge_out_proj",
    )(o_sb, o_mla, gates, h, wsb, wmla, wo, g.reshape(1, -1))


def _ffn_up_kernel(hn_ref, wa_ref, wg_ref, cwa_ref, cwg_ref, cba_ref, cbg_ref, o_ref, bufa, bufg, *, tiles_per_seq):
    i = pl.program_id(1)
    tm = hn_ref.shape[0]
    hn = hn_ref[...]

    @pl.when(i % tiles_per_seq == 0)
    def _():
        bufa[0:8, :] = jnp.zeros((8, bufa.shape[1]), F32)
        bufg[0:8, :] = jnp.zeros((8, bufg.shape[1]), F32)

    def conv(w_ref, cw_ref, cb_ref, buf):
        buf[8:8 + tm, :] = _dot(hn, w_ref[...])
        out = (cb_ref[...] + cw_ref[0:1, :] * buf[6:6 + tm, :] + cw_ref[1:2, :] * buf[7:7 + tm, :]
               + cw_ref[2:3, :] * buf[8:8 + tm, :])
        buf[0:8, :] = buf[tm:tm + 8, :]
        return out

    a = conv(wa_ref, cwa_ref, cba_ref, bufa)
    g = conv(wg_ref, cwg_ref, cbg_ref, bufg)
    o_ref[...] = (a * jax.nn.sigmoid(a) * g).astype(o_ref.dtype)


def _ffn_up(hn, wa, wg, cwa, cwg, cba, cbg, tm, tf, seq_rows):
    m, d = hn.shape
    return pl.pallas_call(
        functools.partial(_ffn_up_kernel, tiles_per_seq=seq_rows // tm),
        grid=(D_FF // tf, m // tm),
        in_specs=[
            pl.BlockSpec((tm, d), lambda j, i: (i, 0)),
            pl.BlockSpec((d, tf), lambda j, i: (0, j)),
            pl.BlockSpec((d, tf), lambda j, i: (0, j)),
            pl.BlockSpec((CONV_W, tf), lambda j, i: (0, j)),
            pl.BlockSpec((CONV_W, tf), lambda j, i: (0, j)),
            pl.BlockSpec((1, tf), lambda j, i: (0, j)),
            pl.BlockSpec((1, tf), lambda j, i: (0, j)),
        ],
        out_specs=pl.BlockSpec((tm, tf), lambda j, i: (i, j)),
        out_shape=jax.ShapeDtypeStruct((m, D_FF), BF16),
        scratch_shapes=[pltpu.VMEM((tm + 8, tf), F32), pltpu.VMEM((tm + 8, tf), F32)],
        compiler_params=_cparams(2),
        name="ffn_up_conv_glu",
    )(hn, wa, wg, cwa, cwg, cba, cbg)


def _ffn_down_kernel(act_ref, h_ref, w_ref, g_ref, *out_refs, final):
    y = h_ref[...] + _dot(act_ref[...], w_ref[...])
    if final:
        out_refs[0][...] = _rms(y, g_ref[...])
    else:
        out_refs[0][...] = y
        out_refs[1][...] = _rms(y, g_ref[...]).astype(out_refs[1].dtype)


def _ffn_down(act, h, w, g, tm, final):
    m = h.shape[0]
    row = lambda i: (i, 0)
    const = lambda i: (0, 0)
    out_spec = pl.BlockSpec((tm, D_MODEL), row)
    if final:
        out_specs, out_shape = [out_spec], [jax.ShapeDtypeStruct((m, D_MODEL), F32)]
    else:
        out_specs = [out_spec, out_spec]
        out_shape = [jax.ShapeDtypeStruct((m, D_MODEL), F32), jax.ShapeDtypeStruct((m, D_MODEL), BF16)]
    return pl.pallas_call(
        functools.partial(_ffn_down_kernel, final=final),
        grid=(m // tm,),
        in_specs=[
            pl.BlockSpec((tm, D_FF), row),
            pl.BlockSpec((tm, D_MODEL), row),
            pl.BlockSpec((D_FF, D_MODEL), const, pipeline_mode=pl.Buffered(1)),
            pl.BlockSpec((1, D_MODEL), const),
        ],
        out_specs=out_specs,
        out_shape=out_shape,
        compiler_params=_cparams(1),
        name="ffn_down",
    )(act, h, w, g.reshape(1, -1))


def _rotate_partner(w):
    half = w.shape[-1] // 2
    return jnp.concatenate([-w[..., half:], w[..., :half]], axis=-1)


def _rope_tables(lp):
    half = ROPE_DIM // 2
    freqs = ROPE_THETA ** (-jnp.arange(half, dtype=F32) / half)
    ang = jnp.arange(lp).astype(F32)[:, None] * freqs[None, :]
    cc = jnp.concatenate([jnp.cos(ang)] * 2, axis=1)
    ss = jnp.concatenate([jnp.sin(ang)] * 2, axis=1)
    pad = jnp.zeros_like(cc)
    return jnp.concatenate([cc, pad], axis=1), jnp.concatenate([ss, pad], axis=1), cc.T, ss.T


def kernel(x, meta_tokens, norm_mix, w_in, q_norm, w_uq, kv_norm, w_ukv, w_sb_out, w_mla_out, w_o, norm_ffn,
           w_up, conv_w, conv_b, w_down, final_norm):
    batch, seq, d = x.shape
    depth = w_in.shape[0]
    l = N_META + seq
    nblk = -(-l // TILE)
    lp = nblk * TILE
    m = batch * lp
    assert d == D_MODEL and m % 768 == 0 and lp % 768 == 0

    meta = jnp.broadcast_to(meta_tokens[None].astype(x.dtype), (batch, N_META, d))
    h = jnp.concatenate([meta, x, jnp.zeros((batch, lp - l, d), x.dtype)], axis=1).reshape(m, d)
    t1, t2, cct, sst = _rope_tables(lp)

    hn = _rmsnorm(h, norm_mix[0], 768)
    out = None
    for i in range(depth):
        wi = w_in[i]
        o1, o2, o3 = SB_W, 2 * SB_W, 3 * SB_W
        o4 = o3 + Q_LORA + KV_LORA
        o5 = o4 + ROPE_DIM
        wq_t = wi[:, :o1].T.astype(BF16)
        wk = wi[:, o1:o2].astype(BF16)
        wv_t = wi[:, o2:o3].T.astype(BF16)
        w_lat = jnp.concatenate([wi[:, o3:o5], _rotate_partner(wi[:, o4:o5])], axis=1).astype(BF16)
        w_gate = wi[:, o5:].astype(BF16)

        uq = w_uq[i].reshape(Q_LORA, MLA_HEADS, NOPE_DIM + ROPE_DIM)
        uq_rope = uq[..., NOPE_DIM:]
        wq_ext_t = jnp.concatenate([uq[..., :NOPE_DIM], uq_rope, _rotate_partner(uq_rope)], axis=-1) \
            .reshape(Q_LORA, MLA_HEADS * MLA_QK_SLOT).T.astype(BF16)
        ukv = w_ukv[i].reshape(KV_LORA, MLA_HEADS, NOPE_DIM + V_DIM)
        wkn = ukv[..., :NOPE_DIM].reshape(KV_LORA, MLA_HEADS * NOPE_DIM).astype(BF16)
        wv_mla_t = ukv[..., NOPE_DIM:].reshape(KV_LORA, MLA_W).T.astype(BF16)

        q_sb_t = _matmul_t(hn, wq_t, 768, SB_W, name="proj_q_sb")
        k_sb = _matmul(hn, wk, BF16, 768, SB_W, name="proj_k_sb")
        v_sb_t = _matmul_t(hn, wv_t, 768, SB_W, name="proj_v_sb")
        lat = _matmul(hn, w_lat, F32, 768, LAT_W, name="proj_latent")
        gates = _matmul(hn, w_gate, BF16, 768, 1024, act="sigmoid", name="proj_gates")

        o_sb = _sb_attn(q_sb_t, k_sb, v_sb_t, batch, nblk)
        q_t, k_cat, v_t = _mla_prep(lat, q_norm[i], kv_norm[i], wq_ext_t, wkn, wv_mla_t, t1, t2, cct, sst, nblk)
        o_mla = _mla_attn(q_t, k_cat, v_t, batch, nblk)

        h, hn = _merge(o_sb, o_mla, gates, h, w_sb_out[i].astype(BF16), w_mla_out[i].astype(BF16),
                       w_o[i].astype(BF16), norm_ffn[i], 256)

        wu = w_up[i]
        act = _ffn_up(hn, wu[:, :D_FF].astype(BF16), wu[:, D_FF:].astype(BF16),
                      conv_w[i][:, :D_FF], conv_w[i][:, D_FF:],
                      conv_b[i][None, :D_FF], conv_b[i][None, D_FF:], 768, 512, lp)
        if i + 1 < depth:
            h, hn = _ffn_down(act, h, w_down[i].astype(BF16), norm_mix[i + 1], 256, final=False)
        else:
            (out,) = _ffn_down(act, h, w_down[i].astype(BF16), final_norm, 256, final=True)

    return out.reshape(batch, lp, d)[:, N_META:l]
```

```python
import functools

import jax
import jax.numpy as jnp
from jax import lax
from jax.experimental import pallas as pl
from jax.experimental.pallas import tpu as pltpu

F32 = jnp.float32
BF16 = jnp.bfloat16

D_MODEL = 2048
N_META = 16
SB_HEADS = 8
SB_HEAD_DIM = 128
SB_W = SB_HEADS * SB_HEAD_DIM
MLA_HEADS = 8
Q_LORA = 512
KV_LORA = 256
NOPE_DIM = 128
ROPE_DIM = 64
V_DIM = 128
MLA_W = MLA_HEADS * V_DIM
MLA_QK_SLOT = 256
MLA_V_SLOT = V_DIM + 16
ROPE_THETA = 10000.0
D_FF = 5632
CONV_W = 3
EPS = 1e-6
LAT_W = Q_LORA + KV_LORA + 2 * ROPE_DIM

TILE = 256
HEAD_GROUP = 4
LOG2_E = 1.4426950408889634
VMEM_LIMIT = 56 * 1024 * 1024
SB_DEAD_LOG = -104.0


def _cparams(n_axes):
    return pltpu.CompilerParams(dimension_semantics=("arbitrary",) * n_axes,
                                vmem_limit_bytes=VMEM_LIMIT)


def _rms(x, g):
    return x * lax.rsqrt(jnp.mean(x * x, axis=-1, keepdims=True) + EPS) * g


def _dot(a, b):
    return jnp.dot(a, b, preferred_element_type=F32)


def _dot_nt(a, b):
    return lax.dot_general(a, b, (((1,), (1,)), ((), ())), preferred_element_type=F32)


def _rmsnorm_kernel(x_ref, g_ref, o_ref):
    o_ref[...] = _rms(x_ref[...], g_ref[...]).astype(o_ref.dtype)


def _rmsnorm(x, g, tm):
    m, d = x.shape
    return pl.pallas_call(
        _rmsnorm_kernel,
        grid=(m // tm,),
        in_specs=[pl.BlockSpec((tm, d), lambda i: (i, 0)), pl.BlockSpec((1, d), lambda i: (0, 0))],
        out_specs=pl.BlockSpec((tm, d), lambda i: (i, 0)),
        out_shape=jax.ShapeDtypeStruct((m, d), BF16),
        compiler_params=_cparams(1),
        name="rmsnorm",
    )(x, g.reshape(1, d))


def _matmul_kernel(x_ref, w_ref, o_ref, *, act):
    y = _dot(x_ref[...], w_ref[...])
    if act == "sigmoid":
        y = jax.nn.sigmoid(y)
    o_ref[...] = y.astype(o_ref.dtype)


def _matmul(x, w, out_dtype, tm, tn, act=None, name="matmul"):
    m, k = x.shape
    n = w.shape[1]
    return pl.pallas_call(
        functools.partial(_matmul_kernel, act=act),
        grid=(n // tn, m // tm),
        in_specs=[pl.BlockSpec((tm, k), lambda j, i: (i, 0)), pl.BlockSpec((k, tn), lambda j, i: (0, j))],
        out_specs=pl.BlockSpec((tm, tn), lambda j, i: (i, j)),
        out_shape=jax.ShapeDtypeStruct((m, n), out_dtype),
        compiler_params=_cparams(2),
        name=name,
    )(x, w)


def _matmul_t_kernel(x_ref, wt_ref, o_ref):
    for r in range(o_ref.shape[0]):
        o_ref[r] = _dot_nt(wt_ref[...], x_ref[r * TILE:(r + 1) * TILE, :]).astype(o_ref.dtype)


def _matmul_t(x, wt, tm, tn, name="matmul_t"):
    m, k = x.shape
    n = wt.shape[0]
    return pl.pallas_call(
        _matmul_t_kernel,
        grid=(n // tn, m // tm),
        in_specs=[pl.BlockSpec((tm, k), lambda j, i: (i, 0)), pl.BlockSpec((tn, k), lambda j, i: (j, 0))],
        out_specs=pl.BlockSpec((tm // TILE, tn, TILE), lambda j, i: (i, j, 0)),
        out_shape=jax.ShapeDtypeStruct((m // TILE, n, TILE), BF16),
        compiler_params=_cparams(2),
        name=name,
    )(x, wt)


def _mla_prep_kernel(lat_ref, gq_ref, gkv_ref, wqt_ref, wkn_ref, wvt_ref, t1_ref, t2_ref, cct_ref, sst_ref,
                     qt_ref, k_ref, vt_ref):
    lat = lat_ref[...]
    cq = _rms(lat[:, :Q_LORA], gq_ref[...]).astype(BF16)
    ckv = _rms(lat[:, Q_LORA:Q_LORA + KV_LORA], gkv_ref[...]).astype(BF16)
    kr = lat[:, Q_LORA + KV_LORA:]

    qt = _dot_nt(wqt_ref[...], cq)
    cct, sst = cct_ref[...], sst_ref[...]
    for h in range(MLA_HEADS):
        base = h * MLA_QK_SLOT
        r0, r1, r2 = base + NOPE_DIM, base + NOPE_DIM + ROPE_DIM, base + MLA_QK_SLOT
        qt_ref[base:r0, :] = qt[base:r0].astype(BF16)
        qt_ref[r0:r1, :] = (qt[r0:r1] * cct + qt[r1:r2] * sst).astype(BF16)
        qt_ref[r1:r2, :] = jnp.zeros((ROPE_DIM, TILE), BF16)

    kn = _dot(ckv, wkn_ref[...])
    k_rot = (kr * t1_ref[...] + pltpu.roll(kr, ROPE_DIM, axis=1) * t2_ref[...]).astype(BF16)
    for h in range(MLA_HEADS):
        base = h * MLA_QK_SLOT
        k_ref[:, base:base + NOPE_DIM] = kn[:, h * NOPE_DIM:(h + 1) * NOPE_DIM].astype(BF16)
        k_ref[:, base + NOPE_DIM:base + MLA_QK_SLOT] = k_rot

    vt = _dot_nt(wvt_ref[...], ckv)
    for h in range(MLA_HEADS):
        base = h * MLA_V_SLOT
        vt_ref[base:base + V_DIM, :] = vt[h * V_DIM:(h + 1) * V_DIM].astype(BF16)
        vt_ref[base + V_DIM:base + MLA_V_SLOT, :] = jnp.ones((MLA_V_SLOT - V_DIM, TILE), BF16)


def _mla_prep(lat, gq, gkv, wqt, wkn, wvt, t1, t2, cct, sst, nblk):
    m = lat.shape[0]
    nt = m // TILE
    hq = MLA_HEADS * MLA_QK_SLOT
    const = lambda i: (0, 0)
    return pl.pallas_call(
        _mla_prep_kernel,
        grid=(nt,),
        in_specs=[
            pl.BlockSpec((TILE, LAT_W), lambda i: (i, 0)),
            pl.BlockSpec((1, Q_LORA), const),
            pl.BlockSpec((1, KV_LORA), const),
            pl.BlockSpec((hq, Q_LORA), const),
            pl.BlockSpec((KV_LORA, MLA_HEADS * NOPE_DIM), const),
            pl.BlockSpec((MLA_W, KV_LORA), const),
            pl.BlockSpec((TILE, 2 * ROPE_DIM), lambda i: (i % nblk, 0)),
            pl.BlockSpec((TILE, 2 * ROPE_DIM), lambda i: (i % nblk, 0)),
            pl.BlockSpec((ROPE_DIM, TILE), lambda i: (0, i % nblk)),
            pl.BlockSpec((ROPE_DIM, TILE), lambda i: (0, i % nblk)),
        ],
        out_specs=[
            pl.BlockSpec((None, hq, TILE), lambda i: (i, 0, 0)),
            pl.BlockSpec((TILE, hq), lambda i: (i, 0)),
            pl.BlockSpec((None, MLA_HEADS * MLA_V_SLOT, TILE), lambda i: (i, 0, 0)),
        ],
        out_shape=[
            jax.ShapeDtypeStruct((nt, hq, TILE), BF16),
            jax.ShapeDtypeStruct((m, hq), BF16),
            jax.ShapeDtypeStruct((nt, MLA_HEADS * MLA_V_SLOT, TILE), BF16),
        ],
        compiler_params=_cparams(1),
        name="mla_prep",
    )(lat, gq.reshape(1, -1), gkv.reshape(1, -1), wqt, wkn, wvt, t1, t2, cct, sst)


def _mla_attn_kernel(qt_ref, k_ref, vt_ref, o_ref, acc_ref, sa_ref, sb_ref):
    qi = pl.program_id(2)
    c = (NOPE_DIM + ROPE_DIM) ** -0.5 * LOG2_E

    def scores(j, s_ref):
        for g in range(HEAD_GROUP):
            qs = slice(g * MLA_QK_SLOT, (g + 1) * MLA_QK_SLOT)
            s_ref[g] = _dot(k_ref[j, :, qs], qt_ref[qs, :])

    def consume(j, s_ref, ms, masked):
        new = []
        for g in range(HEAD_GROUP):
            vs = slice(g * MLA_V_SLOT, (g + 1) * MLA_V_SLOT)
            s = s_ref[g]
            if masked:
                key = lax.broadcasted_iota(jnp.int32, (TILE, TILE), 0)
                qry = lax.broadcasted_iota(jnp.int32, (TILE, TILE), 1)
                s = jnp.where(key <= qry, s, -1e30)
            m_new = jnp.maximum(ms[g], jnp.max(s, axis=0, keepdims=True))
            mc = m_new * c
            alpha = jnp.exp2(ms[g] * c - mc)
            p = jnp.exp2(s * c - mc).astype(BF16)
            acc_ref[g] = alpha * acc_ref[g] + _dot(vt_ref[j, vs, :], p)
            new.append(m_new)
        return tuple(new)

    acc_ref[...] = jnp.zeros(acc_ref.shape, F32)
    scores(0, sa_ref)

    def pair(i, ms):
        j = 2 * i
        scores(j + 1, sb_ref)
        ms = consume(j, sa_ref, ms, False)
        scores(j + 2, sa_ref)
        return consume(j + 1, sb_ref, ms, False)

    init = tuple(jnp.full((1, TILE), -1e30, F32) for _ in range(HEAD_GROUP))
    ms = lax.fori_loop(0, qi // 2, pair, init)

    @pl.when(qi % 2 == 0)
    def _():
        consume(qi, sa_ref, ms, True)

    @pl.when(qi % 2 == 1)
    def _():
        scores(qi, sb_ref)
        consume(qi, sb_ref, consume(qi - 1, sa_ref, ms, False), True)

    for g in range(HEAD_GROUP):
        acc = acc_ref[g]
        o_ref[:, g * V_DIM:(g + 1) * V_DIM] = (acc[:V_DIM] / acc[V_DIM:V_DIM + 1]).T.astype(o_ref.dtype)


def _mla_attn(qt, k_cat, vt, batch, nblk):
    m = k_cat.shape[0]
    k3 = k_cat.reshape(batch * nblk, TILE, MLA_HEADS * MLA_QK_SLOT)
    g = HEAD_GROUP
    once = pl.Buffered(1)
    return pl.pallas_call(
        _mla_attn_kernel,
        grid=(batch, MLA_HEADS // g, nblk),
        in_specs=[
            pl.BlockSpec((None, g * MLA_QK_SLOT, TILE), lambda b, h, q: (b * nblk + q, h, 0)),
            pl.BlockSpec((nblk, TILE, g * MLA_QK_SLOT), lambda b, h, q: (b, 0, h), pipeline_mode=once),
            pl.BlockSpec((nblk, g * MLA_V_SLOT, TILE), lambda b, h, q: (b, h, 0), pipeline_mode=once),
        ],
        out_specs=pl.BlockSpec((TILE, g * V_DIM), lambda b, h, q: (b * nblk + q, h)),
        out_shape=jax.ShapeDtypeStruct((m, MLA_W), BF16),
        scratch_shapes=[pltpu.VMEM((g, MLA_V_SLOT, TILE), F32), pltpu.VMEM((g, TILE, TILE), F32),
                        pltpu.VMEM((g, TILE, TILE), F32)],
        compiler_params=_cparams(3),
        name="mla_attn",
    )(qt, k3, vt)


def _sb_attn_kernel(qt_ref, k_ref, vt_ref, o_ref, acc_ref, z_ref):
    qi = pl.program_id(2)
    scale = SB_HEAD_DIM ** -0.5
    key = lax.broadcasted_iota(jnp.int32, (TILE, TILE), 0)
    col = lax.broadcasted_iota(jnp.int32, (TILE, TILE), 1)
    later_keys = jnp.where(col > key, 1.0, 0.0).astype(BF16)

    def tile(j, carry, masked):
        heads = [slice(g * SB_HEAD_DIM, (g + 1) * SB_HEAD_DIM) for g in range(HEAD_GROUP)]
        valid = key < col
        for g, hs in enumerate(heads):
            z_ref[g] = _dot(k_ref[j, :, hs], qt_ref[hs, :])
        new = []
        for g in range(HEAD_GROUP):
            z = z_ref[g] * scale
            soft = jnp.log(1.0 + jnp.exp(-jnp.abs(z)))
            log_beta = jnp.minimum(z, 0.0) - soft
            log_1m_beta = log_beta - z
            if masked:
                log_1m_beta = jnp.where(valid, log_1m_beta, 0.0)
            hi = log_1m_beta.astype(BF16)
            lo = (log_1m_beta - hi.astype(F32)).astype(BF16)
            later = _dot(later_keys, hi) + _dot(later_keys, lo)
            z_ref[g] = log_beta + later
            new.append(carry[g] + later[0:1] + log_1m_beta[0:1])
        for g, hs in enumerate(heads):
            w = jnp.exp(z_ref[g] + carry[g])
            if masked:
                w = jnp.where(valid, w, 0.0)
            acc_ref[g] += _dot(vt_ref[j, hs, :], w.astype(BF16))
        return tuple(new)

    acc_ref[...] = jnp.zeros(acc_ref.shape, F32)
    carry = tile(qi, tuple(jnp.zeros((1, TILE), F32) for _ in range(HEAD_GROUP)), True)

    def cond(state):
        j, carry = state
        alive = functools.reduce(jnp.maximum, carry)
        return jnp.logical_and(j >= 0, jnp.max(alive) > SB_DEAD_LOG)

    def body(state):
        j, carry = state
        return j - 1, tile(j, carry, False)

    lax.while_loop(cond, body, (qi - 1, carry))
    for g in range(HEAD_GROUP):
        o_ref[:, g * SB_HEAD_DIM:(g + 1) * SB_HEAD_DIM] = acc_ref[g].T.astype(o_ref.dtype)


def _sb_attn(qt, k, vt, batch, nblk):
    m = k.shape[0]
    k3 = k.reshape(batch * nblk, TILE, SB_W)
    g = HEAD_GROUP
    return pl.pallas_call(
        _sb_attn_kernel,
        grid=(batch, SB_HEADS // g, nblk),
        in_specs=[
            pl.BlockSpec((None, g * SB_HEAD_DIM, TILE), lambda b, h, q: (b * nblk + q, h, 0)),
            pl.BlockSpec((nblk, TILE, g * SB_HEAD_DIM), lambda b, h, q: (b, 0, h)),
            pl.BlockSpec((nblk, g * SB_HEAD_DIM, TILE), lambda b, h, q: (b, h, 0)),
        ],
        out_specs=pl.BlockSpec((TILE, g * SB_HEAD_DIM), lambda b, h, q: (b * nblk + q, h)),
        out_shape=jax.ShapeDtypeStruct((m, SB_W), BF16),
        scratch_shapes=[pltpu.VMEM((g, SB_HEAD_DIM, TILE), F32), pltpu.VMEM((g, TILE, TILE), F32)],
        compiler_params=_cparams(3),
        name="sb_attn",
    )(qt, k3, vt)


def _merge_kernel(osb_ref, omla_ref, gate_ref, h_ref, wsb_ref, wmla_ref, wo_ref, g_ref, h_out_ref, hn_out_ref):
    a = _dot(osb_ref[...], wsb_ref[...])
    b = _dot(omla_ref[...], wmla_ref[...])
    merged = gate_ref[:, :D_MODEL].astype(F32) * a + gate_ref[:, D_MODEL:].astype(F32) * b
    y = h_ref[...] + _dot(merged.astype(BF16), wo_ref[...])
    h_out_ref[...] = y
    hn_out_ref[...] = _rms(y, g_ref[...]).astype(hn_out_ref.dtype)


def _merge(o_sb, o_mla, gates, h, wsb, wmla, wo, g, tm):
    m = h.shape[0]
    row = lambda i: (i, 0)
    const = lambda i: (0, 0)
    once = pl.Buffered(1)
    return pl.pallas_call(
        _merge_kernel,
        grid=(m // tm,),
        in_specs=[
            pl.BlockSpec((tm, SB_W), row),
            pl.BlockSpec((tm, MLA_W), row),
            pl.BlockSpec((tm, 2 * D_MODEL), row),
            pl.BlockSpec((tm, D_MODEL), row),
            pl.BlockSpec((SB_W, D_MODEL), const, pipeline_mode=once),
            pl.BlockSpec((MLA_W, D_MODEL), const, pipeline_mode=once),
            pl.BlockSpec((D_MODEL, D_MODEL), const, pipeline_mode=once),
            pl.BlockSpec((1, D_MODEL), const),
        ],
        out_specs=[pl.BlockSpec((tm, D_MODEL), row), pl.BlockSpec((tm, D_MODEL), row)],
        out_shape=[jax.ShapeDtypeStruct((m, D_MODEL), F32), jax.ShapeDtypeStruct((m, D_MODEL), BF16)],
        compiler_params=_cparams(1),
        name="merge_out_proj",
    )(o_sb, o_mla, gates, h, wsb, wmla, wo, g.reshape(1, -1))


def _ffn_up_kernel(hn_ref, wa_ref, wg_ref, cwa_ref, cwg_ref, cba_ref, cbg_ref, o_ref, bufa, bufg, *, tiles_per_seq):
    i = pl.program_id(1)
    tm = hn_ref.shape[0]
    hn = hn_ref[...]

    @pl.when(i % tiles_per_seq == 0)
    def _():
        bufa[0:8, :] = jnp.zeros((8, bufa.shape[1]), F32)
        bufg[0:8, :] = jnp.zeros((8, bufg.shape[1]), F32)

    def conv(w_ref, cw_ref, cb_ref, buf):
        buf[8:8 + tm, :] = _dot(hn, w_ref[...])
        out = (cb_ref[...] + cw_ref[0:1, :] * buf[6:6 + tm, :] + cw_ref[1:2, :] * buf[7:7 + tm, :]
               + cw_ref[2:3, :] * buf[8:8 + tm, :])
        buf[0:8, :] = buf[tm:tm + 8, :]
        return out

    a = conv(wa_ref, cwa_ref, cba_ref, bufa)
    g = conv(wg_ref, cwg_ref, cbg_ref, bufg)
    o_ref[...] = (a * jax.nn.sigmoid(a) * g).astype(o_ref.dtype)


def _ffn_up(hn, wa, wg, cwa, cwg, cba, cbg, tm, tf, seq_rows):
    m, d = hn.shape
    return pl.pallas_call(
        functools.partial(_ffn_up_kernel, tiles_per_seq=seq_rows // tm),
        grid=(D_FF // tf, m // tm),
        in_specs=[
            pl.BlockSpec((tm, d), lambda j, i: (i, 0)),
            pl.BlockSpec((d, tf), lambda j, i: (0, j)),
            pl.BlockSpec((d, tf), lambda j, i: (0, j)),
            pl.BlockSpec((CONV_W, tf), lambda j, i: (0, j)),
            pl.BlockSpec((CONV_W, tf), lambda j, i: (0, j)),
            pl.BlockSpec((1, tf), lambda j, i: (0, j)),
            pl.BlockSpec((1, tf), lambda j, i: (0, j)),
        ],
        out_specs=pl.BlockSpec((tm, tf), lambda j, i: (i, j)),
        out_shape=jax.ShapeDtypeStruct((m, D_FF), BF16),
        scratch_shapes=[pltpu.VMEM((tm + 8, tf), F32), pltpu.VMEM((tm + 8, tf), F32)],
        compiler_params=_cparams(2),
        name="ffn_up_conv_glu",
    )(hn, wa, wg, cwa, cwg, cba, cbg)


def _ffn_down_kernel(act_ref, h_ref, w_ref, g_ref, *out_refs, final):
    y = h_ref[...] + _dot(act_ref[...], w_ref[...])
    if final:
        out_refs[0][...] = _rms(y, g_ref[...])
    else:
        out_refs[0][...] = y
        out_refs[1][...] = _rms(y, g_ref[...]).astype(out_refs[1].dtype)


def _ffn_down(act, h, w, g, tm, final):
    m = h.shape[0]
    row = lambda i: (i, 0)
    const = lambda i: (0, 0)
    out_spec = pl.BlockSpec((tm, D_MODEL), row)
    if final:
        out_specs, out_shape = [out_spec], [jax.ShapeDtypeStruct((m, D_MODEL), F32)]
    else:
        out_specs = [out_spec, out_spec]
        out_shape = [jax.ShapeDtypeStruct((m, D_MODEL), F32), jax.ShapeDtypeStruct((m, D_MODEL), BF16)]
    return pl.pallas_call(
        functools.partial(_ffn_down_kernel, final=final),
        grid=(m // tm,),
        in_specs=[
            pl.BlockSpec((tm, D_FF), row),
            pl.BlockSpec((tm, D_MODEL), row),
            pl.BlockSpec((D_FF, D_MODEL), const, pipeline_mode=pl.Buffered(1)),
            pl.BlockSpec((1, D_MODEL), const),
        ],
        out_specs=out_specs,
        out_shape=out_shape,
        compiler_params=_cparams(1),
        name="ffn_down",
    )(act, h, w, g.reshape(1, -1))


def _rotate_partner(w):
    half = w.shape[-1] // 2
    return jnp.concatenate([-w[..., half:], w[..., :half]], axis=-1)


def _rope_tables(lp):
    half = ROPE_DIM // 2
    freqs = ROPE_THETA ** (-jnp.arange(half, dtype=F32) / half)
    ang = jnp.arange(lp).astype(F32)[:, None] * freqs[None, :]
    cc = jnp.concatenate([jnp.cos(ang)] * 2, axis=1)
    ss = jnp.concatenate([jnp.sin(ang)] * 2, axis=1)
    pad = jnp.zeros_like(cc)
    return jnp.concatenate([cc, pad], axis=1), jnp.concatenate([ss, pad], axis=1), cc.T, ss.T


def kernel(x, meta_tokens, norm_mix, w_in, q_norm, w_uq, kv_norm, w_ukv, w_sb_out, w_mla_out, w_o, norm_ffn,
           w_up, conv_w, conv_b, w_down, final_norm):
    batch, seq, d = x.shape
    depth = w_in.shape[0]
    l = N_META + seq
    nblk = -(-l // TILE)
    lp = nblk * TILE
    m = batch * lp
    assert d == D_MODEL and m % 768 == 0 and lp % 768 == 0

    meta = jnp.broadcast_to(meta_tokens[None].astype(x.dtype), (batch, N_META, d))
    h = jnp.concatenate([meta, x, jnp.zeros((batch, lp - l, d), x.dtype)], axis=1).reshape(m, d)
    t1, t2, cct, sst = _rope_tables(lp)

    hn = _rmsnorm(h, norm_mix[0], 768)
    out = None
    for i in range(depth):
        wi = w_in[i]
        o1, o2, o3 = SB_W, 2 * SB_W, 3 * SB_W
        o4 = o3 + Q_LORA + KV_LORA
        o5 = o4 + ROPE_DIM
        wq_t = wi[:, :o1].T.astype(BF16)
        wk = wi[:, o1:o2].astype(BF16)
        wv_t = wi[:, o2:o3].T.astype(BF16)
        w_lat = jnp.concatenate([wi[:, o3:o5], _rotate_partner(wi[:, o4:o5])], axis=1).astype(BF16)
        w_gate = wi[:, o5:].astype(BF16)

        uq = w_uq[i].reshape(Q_LORA, MLA_HEADS, NOPE_DIM + ROPE_DIM)
        uq_rope = uq[..., NOPE_DIM:]
        wq_ext_t = jnp.concatenate([uq[..., :NOPE_DIM], uq_rope, _rotate_partner(uq_rope)], axis=-1) \
            .reshape(Q_LORA, MLA_HEADS * MLA_QK_SLOT).T.astype(BF16)
        ukv = w_ukv[i].reshape(KV_LORA, MLA_HEADS, NOPE_DIM + V_DIM)
        wkn = ukv[..., :NOPE_DIM].reshape(KV_LORA, MLA_HEADS * NOPE_DIM).astype(BF16)
        wv_mla_t = ukv[..., NOPE_DIM:].reshape(KV_LORA, MLA_W).T.astype(BF16)

        q_sb_t = _matmul_t(hn, wq_t, 768, SB_W, name="proj_q_sb")
        k_sb = _matmul(hn, wk, BF16, 768, SB_W, name="proj_k_sb")
        v_sb_t = _matmul_t(hn, wv_t, 768, SB_W, name="proj_v_sb")
        lat = _matmul(hn, w_lat, F32, 768, LAT_W, name="proj_latent")
        gates = _matmul(hn, w_gate, BF16, 768, 1024, act="sigmoid", name="proj_gates")

        o_sb = _sb_attn(q_sb_t, k_sb, v_sb_t, batch, nblk)
        q_t, k_cat, v_t = _mla_prep(lat, q_norm[i], kv_norm[i], wq_ext_t, wkn, wv_mla_t, t1, t2, cct, sst, nblk)
        o_mla = _mla_attn(q_t, k_cat, v_t, batch, nblk)

        h, hn = _merge(o_sb, o_mla, gates, h, w_sb_out[i].astype(BF16), w_mla_out[i].astype(BF16),
                       w_o[i].astype(BF16), norm_ffn[i], 256)

        wu = w_up[i]
        act = _ffn_up(hn, wu[:, :D_FF].astype(BF16), wu[:, D_FF:].astype(BF16),
                      conv_w[i][:, :D_FF], conv_w[i][:, D_FF:],
                      conv_b[i][None, :D_FF], conv_b[i][None, D_FF:], 768, 512, lp)
        if i + 1 < depth:
            h, hn = _ffn_down(act, h, w_down[i].astype(BF16), norm_mix[i + 1], 256, final=False)
        else:
            (out,) = _ffn_down(act, h, w_down[i].astype(BF16), final_norm, 256, final=True)

    return out.reshape(batch, lp, d)[:, N_META:l]
```

```python
import functools

import jax
import jax.numpy as jnp
from jax import lax
from jax.experimental import pallas as pl
from jax.experimental.pallas import tpu as pltpu

F32 = jnp.float32
BF16 = jnp.bfloat16

D_MODEL = 2048
N_META = 16
SB_HEADS = 8
SB_HEAD_DIM = 128
SB_W = SB_HEADS * SB_HEAD_DIM
MLA_HEADS = 8
Q_LORA = 512
KV_LORA = 256
NOPE_DIM = 128
ROPE_DIM = 64
V_DIM = 128
MLA_W = MLA_HEADS * V_DIM
MLA_QK_SLOT = 256
MLA_V_SLOT = V_DIM + 16
ROPE_THETA = 10000.0
D_FF = 5632
CONV_W = 3
EPS = 1e-6
LAT_W = Q_LORA + KV_LORA + 2 * ROPE_DIM

TILE = 256
HEAD_GROUP = 4
LOG2_E = 1.4426950408889634
MLA_EXP2_SCALE = (NOPE_DIM + ROPE_DIM) ** -0.5 * LOG2_E
VMEM_LIMIT = 56 * 1024 * 1024
SB_DEAD_LOG = -104.0


def _cparams(n_axes):
    return pltpu.CompilerParams(dimension_semantics=("arbitrary",) * n_axes,
                                vmem_limit_bytes=VMEM_LIMIT)


def _rms(x, g):
    return x * lax.rsqrt(jnp.mean(x * x, axis=-1, keepdims=True) + EPS) * g


def _dot(a, b):
    return jnp.dot(a, b, preferred_element_type=F32)


def _dot_nt(a, b):
    return lax.dot_general(a, b, (((1,), (1,)), ((), ())), preferred_element_type=F32)


def _rmsnorm_kernel(x_ref, g_ref, o_ref):
    o_ref[...] = _rms(x_ref[...], g_ref[...]).astype(o_ref.dtype)


def _rmsnorm(x, g, tm):
    m, d = x.shape
    return pl.pallas_call(
        _rmsnorm_kernel,
        grid=(m // tm,),
        in_specs=[pl.BlockSpec((tm, d), lambda i: (i, 0)), pl.BlockSpec((1, d), lambda i: (0, 0))],
        out_specs=pl.BlockSpec((tm, d), lambda i: (i, 0)),
        out_shape=jax.ShapeDtypeStruct((m, d), BF16),
        compiler_params=_cparams(1),
        name="rmsnorm",
    )(x, g.reshape(1, d))


def _proj_kernel(x_ref, w_ref, o_ref, *scratch, act, transposed):
    k = w_ref.shape[0]
    if scratch:
        (w_bf,) = scratch

        @pl.when(pl.program_id(1) == 0)
        def _():
            if transposed:
                for c in range(k // TILE):
                    rows = slice(c * TILE, (c + 1) * TILE)
                    w_bf[:, rows] = w_ref[rows, :].T.astype(BF16)
            else:
                w_bf[...] = w_ref[...].astype(BF16)
    else:
        w_bf = w_ref

    if transposed:
        for r in range(o_ref.shape[0]):
            o_ref[r] = _dot_nt(w_bf[...], x_ref[r * TILE:(r + 1) * TILE, :]).astype(o_ref.dtype)
    else:
        for r in range(x_ref.shape[0] // TILE):
            rows = slice(r * TILE, (r + 1) * TILE)
            y = _dot(x_ref[rows, :], w_bf[...])
            if act == "sigmoid":
                y = jax.nn.sigmoid(y)
            o_ref[rows, :] = y.astype(o_ref.dtype)


def _proj(x, w, layer, col0, n, out_dtype, tm, tn, act=None, transposed=False, name="proj"):
    m, k = x.shape
    assert col0 % tn == 0 and n % tn == 0 and m % tm == 0 and tm % TILE == 0
    cast = w.dtype != BF16
    assert cast or not transposed
    if transposed:
        out_spec = pl.BlockSpec((tm // TILE, tn, TILE), lambda j, i: (i, j, 0))
        out_shape = jax.ShapeDtypeStruct((m // TILE, n, TILE), out_dtype)
        scratch = [pltpu.VMEM((tn, k), BF16)]
    else:
        out_spec = pl.BlockSpec((tm, tn), lambda j, i: (i, j))
        out_shape = jax.ShapeDtypeStruct((m, n), out_dtype)
        scratch = [pltpu.VMEM((k, tn), BF16)] if cast else []
    return pl.pallas_call(
        functools.partial(_proj_kernel, act=act, transposed=transposed),
        grid=(n // tn, m // tm),
        in_specs=[pl.BlockSpec((tm, k), lambda j, i: (i, 0)),
                  pl.BlockSpec((None, k, tn), lambda j, i: (layer, 0, col0 // tn + j))],
        out_specs=out_spec,
        out_shape=out_shape,
        scratch_shapes=scratch,
        compiler_params=_cparams(2),
        name=name,
    )(x, w)


def _mla_prep_kernel(lat_ref, gq_ref, gkv_ref, wqt_ref, wkn_ref, wvt_ref, t1_ref, t2_ref, cct_ref, sst_ref,
                     qt_ref, k_ref, vt_ref):
    lat = lat_ref[...]
    cq = _rms(lat[:, :Q_LORA], gq_ref[...]).astype(BF16)
    ckv = _rms(lat[:, Q_LORA:Q_LORA + KV_LORA], gkv_ref[...]).astype(BF16)
    kr = lat[:, Q_LORA + KV_LORA:]

    qt = _dot_nt(wqt_ref[...], cq)
    cct, sst = cct_ref[...], sst_ref[...]
    for h in range(MLA_HEADS):
        base = h * MLA_QK_SLOT
        r0, r1, r2 = base + NOPE_DIM, base + NOPE_DIM + ROPE_DIM, base + MLA_QK_SLOT
        qt_ref[base:r0, :] = qt[base:r0].astype(BF16)
        qt_ref[r0:r1, :] = (qt[r0:r1] * cct + qt[r1:r2] * sst).astype(BF16)
        qt_ref[r1:r2, :] = jnp.zeros((ROPE_DIM, TILE), BF16)

    kn = _dot(ckv, wkn_ref[...]) * MLA_EXP2_SCALE
    k_rot = ((kr * t1_ref[...] + pltpu.roll(kr, ROPE_DIM, axis=1) * t2_ref[...]) * MLA_EXP2_SCALE).astype(BF16)
    for h in range(MLA_HEADS):
        base = h * MLA_QK_SLOT
        k_ref[:, base:base + NOPE_DIM] = kn[:, h * NOPE_DIM:(h + 1) * NOPE_DIM].astype(BF16)
        k_ref[:, base + NOPE_DIM:base + MLA_QK_SLOT] = k_rot

    vt = _dot_nt(wvt_ref[...], ckv)
    for h in range(MLA_HEADS):
        base = h * MLA_V_SLOT
        vt_ref[base:base + V_DIM, :] = vt[h * V_DIM:(h + 1) * V_DIM].astype(BF16)
        vt_ref[base + V_DIM:base + MLA_V_SLOT, :] = jnp.ones((MLA_V_SLOT - V_DIM, TILE), BF16)


def _mla_prep(lat, gq, gkv, wqt, wkn, wvt, t1, t2, cct, sst, nblk):
    m = lat.shape[0]
    nt = m // TILE
    hq = MLA_HEADS * MLA_QK_SLOT
    const = lambda i: (0, 0)
    return pl.pallas_call(
        _mla_prep_kernel,
        grid=(nt,),
        in_specs=[
            pl.BlockSpec((TILE, LAT_W), lambda i: (i, 0)),
            pl.BlockSpec((1, Q_LORA), const),
            pl.BlockSpec((1, KV_LORA), const),
            pl.BlockSpec((hq, Q_LORA), const),
            pl.BlockSpec((KV_LORA, MLA_HEADS * NOPE_DIM), const),
            pl.BlockSpec((MLA_W, KV_LORA), const),
            pl.BlockSpec((TILE, 2 * ROPE_DIM), lambda i: (i % nblk, 0)),
            pl.BlockSpec((TILE, 2 * ROPE_DIM), lambda i: (i % nblk, 0)),
            pl.BlockSpec((ROPE_DIM, TILE), lambda i: (0, i % nblk)),
            pl.BlockSpec((ROPE_DIM, TILE), lambda i: (0, i % nblk)),
        ],
        out_specs=[
            pl.BlockSpec((None, hq, TILE), lambda i: (i, 0, 0)),
            pl.BlockSpec((TILE, hq), lambda i: (i, 0)),
            pl.BlockSpec((None, MLA_HEADS * MLA_V_SLOT, TILE), lambda i: (i, 0, 0)),
        ],
        out_shape=[
            jax.ShapeDtypeStruct((nt, hq, TILE), BF16),
            jax.ShapeDtypeStruct((m, hq), BF16),
            jax.ShapeDtypeStruct((nt, MLA_HEADS * MLA_V_SLOT, TILE), BF16),
        ],
        compiler_params=_cparams(1),
        name="mla_prep",
    )(lat, gq.reshape(1, -1), gkv.reshape(1, -1), wqt, wkn, wvt, t1, t2, cct, sst)


def _mla_attn_kernel(qt_ref, k_ref, vt_ref, o_ref, acc_ref, sa_ref, sb_ref):
    qi = pl.program_id(2)

    def scores(j, s_ref):
        for g in range(HEAD_GROUP):
            qs = slice(g * MLA_QK_SLOT, (g + 1) * MLA_QK_SLOT)
            s_ref[g] = _dot(k_ref[j, :, qs], qt_ref[qs, :])

    def consume(j, s_ref, ms, masked):
        new = []
        for g in range(HEAD_GROUP):
            vs = slice(g * MLA_V_SLOT, (g + 1) * MLA_V_SLOT)
            s = s_ref[g]
            if masked:
                key = lax.broadcasted_iota(jnp.int32, (TILE, TILE), 0)
                qry = lax.broadcasted_iota(jnp.int32, (TILE, TILE), 1)
                s = jnp.where(key <= qry, s, -1e30)
            m_new = jnp.maximum(ms[g], jnp.max(s, axis=0, keepdims=True))
            alpha = jnp.exp2(ms[g] - m_new)
            p = jnp.exp2(s - m_new).astype(BF16)
            acc_ref[g] = alpha * acc_ref[g] + _dot(vt_ref[j, vs, :], p)
            new.append(m_new)
        return tuple(new)

    acc_ref[...] = jnp.zeros(acc_ref.shape, F32)
    scores(0, sa_ref)

    def pair(i, ms):
        j = 2 * i
        scores(j + 1, sb_ref)
        ms = consume(j, sa_ref, ms, False)
        scores(j + 2, sa_ref)
        return consume(j + 1, sb_ref, ms, False)

    init = tuple(jnp.full((1, TILE), -1e30, F32) for _ in range(HEAD_GROUP))
    ms = lax.fori_loop(0, qi // 2, pair, init)

    @pl.when(qi % 2 == 0)
    def _():
        consume(qi, sa_ref, ms, True)

    @pl.when(qi % 2 == 1)
    def _():
        scores(qi, sb_ref)
        consume(qi, sb_ref, consume(qi - 1, sa_ref, ms, False), True)

    for g in range(HEAD_GROUP):
        acc = acc_ref[g]
        o_ref[:, g * V_DIM:(g + 1) * V_DIM] = (acc[:V_DIM] / acc[V_DIM:V_DIM + 1]).T.astype(o_ref.dtype)


def _mla_attn(qt, k_cat, vt, batch, nblk):
    m = k_cat.shape[0]
    k3 = k_cat.reshape(batch * nblk, TILE, MLA_HEADS * MLA_QK_SLOT)
    g = HEAD_GROUP
    once = pl.Buffered(1)
    return pl.pallas_call(
        _mla_attn_kernel,
        grid=(batch, MLA_HEADS // g, nblk),
        in_specs=[
            pl.BlockSpec((None, g * MLA_QK_SLOT, TILE), lambda b, h, q: (b * nblk + q, h, 0)),
            pl.BlockSpec((nblk, TILE, g * MLA_QK_SLOT), lambda b, h, q: (b, 0, h), pipeline_mode=once),
            pl.BlockSpec((nblk, g * MLA_V_SLOT, TILE), lambda b, h, q: (b, h, 0), pipeline_mode=once),
        ],
        out_specs=pl.BlockSpec((TILE, g * V_DIM), lambda b, h, q: (b * nblk + q, h)),
        out_shape=jax.ShapeDtypeStruct((m, MLA_W), BF16),
        scratch_shapes=[pltpu.VMEM((g, MLA_V_SLOT, TILE), F32), pltpu.VMEM((g, TILE, TILE), F32),
                        pltpu.VMEM((g, TILE, TILE), F32)],
        compiler_params=_cparams(3),
        name="mla_attn",
    )(qt, k3, vt)


def _sb_attn_kernel(qt_ref, k_ref, vt_ref, o_ref, acc_ref, z_ref):
    qi = pl.program_id(2)
    scale = SB_HEAD_DIM ** -0.5
    key = lax.broadcasted_iota(jnp.int32, (TILE, TILE), 0)
    col = lax.broadcasted_iota(jnp.int32, (TILE, TILE), 1)
    later_keys = jnp.where(col > key, 1.0, 0.0).astype(BF16)

    def tile(j, carry, masked):
        heads = [slice(g * SB_HEAD_DIM, (g + 1) * SB_HEAD_DIM) for g in range(HEAD_GROUP)]
        valid = key < col
        for g, hs in enumerate(heads):
            z_ref[g] = _dot(k_ref[j, :, hs], qt_ref[hs, :])
        new = []
        for g in range(HEAD_GROUP):
            z = z_ref[g] * scale
            soft = jnp.log(1.0 + jnp.exp(-jnp.abs(z)))
            log_beta = jnp.minimum(z, 0.0) - soft
            log_1m_beta = log_beta - z
            if masked:
                log_1m_beta = jnp.where(valid, log_1m_beta, 0.0)
            hi = log_1m_beta.astype(BF16)
            lo = (log_1m_beta - hi.astype(F32)).astype(BF16)
            later = _dot(later_keys, hi) + _dot(later_keys, lo)
            z_ref[g] = log_beta + later
            new.append(carry[g] + later[0:1] + log_1m_beta[0:1])
        for g, hs in enumerate(heads):
            w = jnp.exp(z_ref[g] + carry[g])
            if masked:
                w = jnp.where(valid, w, 0.0)
            acc_ref[g] += _dot(vt_ref[j, hs, :], w.astype(BF16))
        return tuple(new)

    acc_ref[...] = jnp.zeros(acc_ref.shape, F32)
    carry = tile(qi, tuple(jnp.zeros((1, TILE), F32) for _ in range(HEAD_GROUP)), True)

    def cond(state):
        j, carry = state
        alive = functools.reduce(jnp.maximum, carry)
        return jnp.logical_and(j >= 0, jnp.max(alive) > SB_DEAD_LOG)

    def body(state):
        j, carry = state
        return j - 1, tile(j, carry, False)

    lax.while_loop(cond, body, (qi - 1, carry))
    for g in range(HEAD_GROUP):
        o_ref[:, g * SB_HEAD_DIM:(g + 1) * SB_HEAD_DIM] = acc_ref[g].T.astype(o_ref.dtype)


def _sb_attn(qt, k, vt, batch, nblk):
    m = k.shape[0]
    k3 = k.reshape(batch * nblk, TILE, SB_W)
    g = HEAD_GROUP
    return pl.pallas_call(
        _sb_attn_kernel,
        grid=(batch, SB_HEADS // g, nblk),
        in_specs=[
            pl.BlockSpec((None, g * SB_HEAD_DIM, TILE), lambda b, h, q: (b * nblk + q, h, 0)),
            pl.BlockSpec((nblk, TILE, g * SB_HEAD_DIM), lambda b, h, q: (b, 0, h)),
            pl.BlockSpec((nblk, g * SB_HEAD_DIM, TILE), lambda b, h, q: (b, h, 0)),
        ],
        out_specs=pl.BlockSpec((TILE, g * SB_HEAD_DIM), lambda b, h, q: (b * nblk + q, h)),
        out_shape=jax.ShapeDtypeStruct((m, SB_W), BF16),
        scratch_shapes=[pltpu.VMEM((g, SB_HEAD_DIM, TILE), F32), pltpu.VMEM((g, TILE, TILE), F32)],
        compiler_params=_cparams(3),
        name="sb_attn",
    )(qt, k3, vt)


def _merge_kernel(osb_ref, omla_ref, gate_ref, h_ref, wsb_ref, wmla_ref, wo_ref, g_ref, h_out_ref, hn_out_ref):
    a = _dot(osb_ref[...], wsb_ref[...])
    b = _dot(omla_ref[...], wmla_ref[...])
    merged = gate_ref[:, :D_MODEL].astype(F32) * a + gate_ref[:, D_MODEL:].astype(F32) * b
    y = h_ref[...] + _dot(merged.astype(BF16), wo_ref[...])
    h_out_ref[...] = y
    hn_out_ref[...] = _rms(y, g_ref[...]).astype(hn_out_ref.dtype)


def _merge(o_sb, o_mla, gates, h, wsb, wmla, wo, layer, g, tm):
    m = h.shape[0]
    row = lambda i: (i, 0)
    const = lambda i: (0, 0)
    this_layer = lambda i: (layer, 0, 0)
    once = pl.Buffered(1)
    return pl.pallas_call(
        _merge_kernel,
        grid=(m // tm,),
        in_specs=[
            pl.BlockSpec((tm, SB_W), row),
            pl.BlockSpec((tm, MLA_W), row),
            pl.BlockSpec((tm, 2 * D_MODEL), row),
            pl.BlockSpec((tm, D_MODEL), row),
            pl.BlockSpec((None, SB_W, D_MODEL), this_layer, pipeline_mode=once),
            pl.BlockSpec((None, MLA_W, D_MODEL), this_layer, pipeline_mode=once),
            pl.BlockSpec((None, D_MODEL, D_MODEL), this_layer, pipeline_mode=once),
            pl.BlockSpec((1, D_MODEL), const),
        ],
        out_specs=[pl.BlockSpec((tm, D_MODEL), row), pl.BlockSpec((tm, D_MODEL), row)],
        out_shape=[jax.ShapeDtypeStruct((m, D_MODEL), F32), jax.ShapeDtypeStruct((m, D_MODEL), BF16)],
        compiler_params=_cparams(1),
        name="merge_out_proj",
    )(o_sb, o_mla, gates, h, wsb, wmla, wo, g.reshape(1, -1))


def _ffn_up_kernel(hn_ref, wa_ref, wg_ref, cwa_ref, cwg_ref, cba_ref, cbg_ref, o_ref, wa_bf, wg_bf, bufa, bufg, *,
                   tiles_per_seq):
    i = pl.program_id(1)
    tm, tf = o_ref.shape

    @pl.when(i == 0)
    def _():
        wa_bf[...] = wa_ref[...].astype(BF16)
        wg_bf[...] = wg_ref[...].astype(BF16)

    @pl.when(i % tiles_per_seq == 0)
    def _():
        bufa[0:8, :] = jnp.zeros((8, tf), F32)
        bufg[0:8, :] = jnp.zeros((8, tf), F32)

    def conv(w_bf, cw_ref, cb_ref, buf):
        buf[8:8 + tm, :] = _dot(hn_ref[...], w_bf[...])
        out = (cb_ref[...] + cw_ref[0:1, :] * buf[6:6 + tm, :] + cw_ref[1:2, :] * buf[7:7 + tm, :]
               + cw_ref[2:3, :] * buf[8:8 + tm, :])
        buf[0:8, :] = buf[tm:tm + 8, :]
        return out

    a = conv(wa_bf, cwa_ref, cba_ref, bufa)
    g = conv(wg_bf, cwg_ref, cbg_ref, bufg)
    o_ref[...] = (a * jax.nn.sigmoid(a) * g).astype(o_ref.dtype)


def _ffn_up(hn, w_up, conv_w, conv_b, layer, tm, tf, seq_rows):
    m, d = hn.shape
    nf = D_FF // tf
    conv_b3 = conv_b.reshape(conv_b.shape[0], 1, 2 * D_FF)
    a_cols = lambda j, i: (layer, 0, j)
    g_cols = lambda j, i: (layer, 0, nf + j)
    buf = pltpu.VMEM((tm + 8, tf), F32)
    return pl.pallas_call(
        functools.partial(_ffn_up_kernel, tiles_per_seq=seq_rows // tm),
        grid=(nf, m // tm),
        in_specs=[
            pl.BlockSpec((tm, d), lambda j, i: (i, 0)),
            pl.BlockSpec((None, d, tf), a_cols),
            pl.BlockSpec((None, d, tf), g_cols),
            pl.BlockSpec((None, CONV_W, tf), a_cols),
            pl.BlockSpec((None, CONV_W, tf), g_cols),
            pl.BlockSpec((None, 1, tf), a_cols),
            pl.BlockSpec((None, 1, tf), g_cols),
        ],
        out_specs=pl.BlockSpec((tm, tf), lambda j, i: (i, j)),
        out_shape=jax.ShapeDtypeStruct((m, D_FF), BF16),
        scratch_shapes=[pltpu.VMEM((d, tf), BF16), pltpu.VMEM((d, tf), BF16), buf, buf],
        compiler_params=_cparams(2),
        name="ffn_up_conv_glu",
    )(hn, w_up, w_up, conv_w, conv_w, conv_b3, conv_b3)


def _ffn_down_kernel(act_ref, h_ref, w_ref, g_ref, *out_refs, final):
    y = h_ref[...] + _dot(act_ref[...], w_ref[...])
    if final:
        out_refs[0][...] = _rms(y, g_ref[...])
    else:
        out_refs[0][...] = y
        out_refs[1][...] = _rms(y, g_ref[...]).astype(out_refs[1].dtype)


def _ffn_down(act, h, w, layer, g, tm, final):
    m = h.shape[0]
    row = lambda i: (i, 0)
    const = lambda i: (0, 0)
    out_spec = pl.BlockSpec((tm, D_MODEL), row)
    if final:
        out_specs, out_shape = [out_spec], [jax.ShapeDtypeStruct((m, D_MODEL), F32)]
    else:
        out_specs = [out_spec, out_spec]
        out_shape = [jax.ShapeDtypeStruct((m, D_MODEL), F32), jax.ShapeDtypeStruct((m, D_MODEL), BF16)]
    return pl.pallas_call(
        functools.partial(_ffn_down_kernel, final=final),
        grid=(m // tm,),
        in_specs=[
            pl.BlockSpec((tm, D_FF), row),
            pl.BlockSpec((tm, D_MODEL), row),
            pl.BlockSpec((None, D_FF, D_MODEL), lambda i: (layer, 0, 0), pipeline_mode=pl.Buffered(1)),
            pl.BlockSpec((1, D_MODEL), const),
        ],
        out_specs=out_specs,
        out_shape=out_shape,
        compiler_params=_cparams(1),
        name="ffn_down",
    )(act, h, w, g.reshape(1, -1))


def _rotate_partner(w):
    half = w.shape[-1] // 2
    return jnp.concatenate([-w[..., half:], w[..., :half]], axis=-1)


def _rope_tables(lp):
    half = ROPE_DIM // 2
    freqs = ROPE_THETA ** (-jnp.arange(half, dtype=F32) / half)
    ang = jnp.arange(lp).astype(F32)[:, None] * freqs[None, :]
    cc = jnp.concatenate([jnp.cos(ang)] * 2, axis=1)
    ss = jnp.concatenate([jnp.sin(ang)] * 2, axis=1)
    pad = jnp.zeros_like(cc)
    return jnp.concatenate([cc, pad], axis=1), jnp.concatenate([ss, pad], axis=1), cc.T, ss.T


def kernel(x, meta_tokens, norm_mix, w_in, q_norm, w_uq, kv_norm, w_ukv, w_sb_out, w_mla_out, w_o, norm_ffn,
           w_up, conv_w, conv_b, w_down, final_norm):
    batch, seq, d = x.shape
    depth = w_in.shape[0]
    l = N_META + seq
    nblk = -(-l // TILE)
    lp = nblk * TILE
    m = batch * lp
    tm = 3 * TILE
    assert d == D_MODEL and m % tm == 0 and lp % tm == 0

    meta = jnp.broadcast_to(meta_tokens[None].astype(x.dtype), (batch, N_META, d))
    h = jnp.concatenate([meta, x, jnp.zeros((batch, lp - l, d), x.dtype)], axis=1).reshape(m, d)
    t1, t2, cct, sst = _rope_tables(lp)

    o3 = 3 * SB_W
    o4 = o3 + Q_LORA + KV_LORA
    o5 = o4 + ROPE_DIM
    w_lat = jnp.concatenate([w_in[:, :, o3:o5], _rotate_partner(w_in[:, :, o4:o5])], axis=2).astype(BF16)
    w_gate = w_in[:, :, o5:].astype(BF16)
    w_sb_out, w_mla_out, w_o, w_down = (w.astype(BF16) for w in (w_sb_out, w_mla_out, w_o, w_down))

    hn = _rmsnorm(h, norm_mix[0], tm)
    out = None
    for i in range(depth):
        uq = w_uq[i].reshape(Q_LORA, MLA_HEADS, NOPE_DIM + ROPE_DIM)
        uq_rope = uq[..., NOPE_DIM:]
        wq_ext_t = jnp.concatenate([uq[..., :NOPE_DIM], uq_rope, _rotate_partner(uq_rope)], axis=-1) \
            .reshape(Q_LORA, MLA_HEADS * MLA_QK_SLOT).T.astype(BF16)
        ukv = w_ukv[i].reshape(KV_LORA, MLA_HEADS, NOPE_DIM + V_DIM)
        wkn = ukv[..., :NOPE_DIM].reshape(KV_LORA, MLA_HEADS * NOPE_DIM).astype(BF16)
        wv_mla_t = ukv[..., NOPE_DIM:].reshape(KV_LORA, MLA_W).T.astype(BF16)

        q_sb_t = _proj(hn, w_in, i, 0, SB_W, BF16, tm, SB_W, transposed=True, name="proj_q_sb")
        k_sb = _proj(hn, w_in, i, SB_W, SB_W, BF16, tm, SB_W, name="proj_k_sb")
        v_sb_t = _proj(hn, w_in, i, 2 * SB_W, SB_W, BF16, tm, SB_W, transposed=True, name="proj_v_sb")
        lat = _proj(hn, w_lat, i, 0, LAT_W, F32, tm, LAT_W, name="proj_latent")
        gates = _proj(hn, w_gate, i, 0, 2 * D_MODEL, BF16, tm, 1024, act="sigmoid", name="proj_gates")

        o_sb = _sb_attn(q_sb_t, k_sb, v_sb_t, batch, nblk)
        q_t, k_cat, v_t = _mla_prep(lat, q_norm[i], kv_norm[i], wq_ext_t, wkn, wv_mla_t, t1, t2, cct, sst, nblk)
        o_mla = _mla_attn(q_t, k_cat, v_t, batch, nblk)

        h, hn = _merge(o_sb, o_mla, gates, h, w_sb_out, w_mla_out, w_o, i, norm_ffn[i], TILE)

        act = _ffn_up(hn, w_up, conv_w, conv_b, i, tm, 512, lp)
        if i + 1 < depth:
            h, hn = _ffn_down(act, h, w_down, i, norm_mix[i + 1], TILE, final=False)
        else:
            (out,) = _ffn_down(act, h, w_down, i, final_norm, TILE, final=True)

    return out.reshape(batch, lp, d)[:, N_META:l]
```

```python
import functools

import jax
import jax.numpy as jnp
from jax import lax
from jax.experimental import pallas as pl
from jax.experimental.pallas import tpu as pltpu

F32 = jnp.float32
BF16 = jnp.bfloat16

D_MODEL = 2048
N_META = 16
SB_HEADS = 8
SB_HEAD_DIM = 128
SB_W = SB_HEADS * SB_HEAD_DIM
MLA_HEADS = 8
Q_LORA = 512
KV_LORA = 256
NOPE_DIM = 128
ROPE_DIM = 64
V_DIM = 128
MLA_W = MLA_HEADS * V_DIM
MLA_QK_SLOT = 256
MLA_V_SLOT = V_DIM + 16
ROPE_THETA = 10000.0
D_FF = 5632
CONV_W = 3
EPS = 1e-6
LAT_W = Q_LORA + KV_LORA + 2 * ROPE_DIM

TILE = 256
HEAD_GROUP = 4
LOG2_E = 1.4426950408889634
MLA_EXP2_SCALE = (NOPE_DIM + ROPE_DIM) ** -0.5 * LOG2_E
VMEM_LIMIT = 56 * 1024 * 1024
SB_DEAD_LOG = -104.0


def _cparams(n_axes):
    return pltpu.CompilerParams(dimension_semantics=("arbitrary",) * n_axes,
                                vmem_limit_bytes=VMEM_LIMIT)


def _rms(x, g):
    return x * lax.rsqrt(jnp.mean(x * x, axis=-1, keepdims=True) + EPS) * g


def _dot(a, b):
    return jnp.dot(a, b, preferred_element_type=F32)


def _dot_nt(a, b):
    return lax.dot_general(a, b, (((1,), (1,)), ((), ())), preferred_element_type=F32)


def _embed_kernel(x_ref, meta_ref, g_ref, h_ref, hn_ref, *, nblk):
    t = pl.program_id(0) % nblk
    x = x_ref[...]
    moved = pltpu.roll(x, N_META, axis=0)
    head = lax.broadcasted_iota(jnp.int32, x.shape, 0) < N_META
    first = jnp.where(head, meta_ref[...], moved)
    last = jnp.where(head, moved, 0.0)
    h = jnp.where(t == 0, first, jnp.where(t == nblk - 1, last, x))
    h_ref[...] = h
    hn_ref[...] = _rms(h, g_ref[...]).astype(hn_ref.dtype)


def _embed(x2, meta_tile, g, batch, seq, nblk):
    d = x2.shape[1]
    m = batch * nblk * TILE

    def start(i):
        tok = jnp.clip((i % nblk) * TILE - N_META, 0, seq - TILE)
        return pl.multiple_of((i // nblk) * seq + tok, 8), 0

    row = lambda i: (i, 0)
    const = lambda i: (0, 0)
    return pl.pallas_call(
        functools.partial(_embed_kernel, nblk=nblk),
        grid=(batch * nblk,),
        in_specs=[pl.BlockSpec((pl.Element(TILE), pl.Element(d)), start),
                  pl.BlockSpec((TILE, d), const), pl.BlockSpec((1, d), const)],
        out_specs=[pl.BlockSpec((TILE, d), row), pl.BlockSpec((TILE, d), row)],
        out_shape=[jax.ShapeDtypeStruct((m, d), F32), jax.ShapeDtypeStruct((m, d), BF16)],
        compiler_params=_cparams(1),
        name="embed_rmsnorm",
    )(x2, meta_tile, g.reshape(1, d))


def _proj_kernel(x_ref, w_ref, o_ref, *scratch, act, transposed):
    k = w_ref.shape[0]
    if scratch:
        (w_bf,) = scratch

        @pl.when(pl.program_id(1) == 0)
        def _():
            if transposed:
                for c in range(k // TILE):
                    rows = slice(c * TILE, (c + 1) * TILE)
                    w_bf[:, rows] = w_ref[rows, :].astype(F32).T.astype(BF16)
            else:
                w_bf[...] = w_ref[...].astype(BF16)
    else:
        w_bf = w_ref

    if transposed:
        for r in range(o_ref.shape[0]):
            o_ref[r] = _dot_nt(w_bf[...], x_ref[r * TILE:(r + 1) * TILE, :]).astype(o_ref.dtype)
    else:
        for r in range(x_ref.shape[0] // TILE):
            rows = slice(r * TILE, (r + 1) * TILE)
            y = _dot(x_ref[rows, :], w_bf[...])
            if act == "sigmoid":
                y = jax.nn.sigmoid(y)
            o_ref[rows, :] = y.astype(o_ref.dtype)


def _proj(x, w, layer, col0, n, out_dtype, tm, tn, act=None, transposed=False, name="proj"):
    m, k = x.shape
    assert col0 % tn == 0 and n % tn == 0 and m % tm == 0 and tm % TILE == 0
    cast = w.dtype != BF16
    if transposed:
        out_spec = pl.BlockSpec((tm // TILE, tn, TILE), lambda j, i: (i, j, 0))
        out_shape = jax.ShapeDtypeStruct((m // TILE, n, TILE), out_dtype)
        scratch = [pltpu.VMEM((tn, k), BF16)]
    else:
        out_spec = pl.BlockSpec((tm, tn), lambda j, i: (i, j))
        out_shape = jax.ShapeDtypeStruct((m, n), out_dtype)
        scratch = [pltpu.VMEM((k, tn), BF16)] if cast else []
    return pl.pallas_call(
        functools.partial(_proj_kernel, act=act, transposed=transposed),
        grid=(n // tn, m // tm),
        in_specs=[pl.BlockSpec((tm, k), lambda j, i: (i, 0)),
                  pl.BlockSpec((None, k, tn), lambda j, i: (layer, 0, col0 // tn + j))],
        out_specs=out_spec,
        out_shape=out_shape,
        scratch_shapes=scratch,
        compiler_params=_cparams(2),
        name=name,
    )(x, w)


def _mla_prep_kernel(hn_ref, wlat_ref, gq_ref, gkv_ref, wqt_ref, wkn_ref, wvt_ref, t1_ref, t2_ref, cct_ref, sst_ref,
                     qt_ref, k_ref, vt_ref):
    lat = _dot(hn_ref[...], wlat_ref[...])
    cq = _rms(lat[:, :Q_LORA], gq_ref[...]).astype(BF16)
    ckv = _rms(lat[:, Q_LORA:Q_LORA + KV_LORA], gkv_ref[...]).astype(BF16)
    kr = lat[:, Q_LORA + KV_LORA:]

    qt = _dot_nt(wqt_ref[...], cq)
    cct, sst = cct_ref[...], sst_ref[...]
    for h in range(MLA_HEADS):
        base = h * MLA_QK_SLOT
        r0, r1, r2 = base + NOPE_DIM, base + NOPE_DIM + ROPE_DIM, base + MLA_QK_SLOT
        qt_ref[base:r0, :] = qt[base:r0].astype(BF16)
        qt_ref[r0:r1, :] = (qt[r0:r1] * cct + qt[r1:r2] * sst).astype(BF16)
        qt_ref[r1:r2, :] = jnp.zeros((ROPE_DIM, TILE), BF16)

    kn = _dot(ckv, wkn_ref[...]) * MLA_EXP2_SCALE
    k_rot = ((kr * t1_ref[...] + pltpu.roll(kr, ROPE_DIM, axis=1) * t2_ref[...]) * MLA_EXP2_SCALE).astype(BF16)
    for h in range(MLA_HEADS):
        base = h * MLA_QK_SLOT
        k_ref[:, base:base + NOPE_DIM] = kn[:, h * NOPE_DIM:(h + 1) * NOPE_DIM].astype(BF16)
        k_ref[:, base + NOPE_DIM:base + MLA_QK_SLOT] = k_rot

    vt = _dot_nt(wvt_ref[...], ckv)
    for h in range(MLA_HEADS):
        base = h * MLA_V_SLOT
        vt_ref[base:base + V_DIM, :] = vt[h * V_DIM:(h + 1) * V_DIM].astype(BF16)
        vt_ref[base + V_DIM:base + MLA_V_SLOT, :] = jnp.ones((MLA_V_SLOT - V_DIM, TILE), BF16)


def _mla_prep(hn, w_lat, layer, gq, gkv, wqt, wkn, wvt, t1, t2, cct, sst, nblk):
    m, d = hn.shape
    nt = m // TILE
    hq = MLA_HEADS * MLA_QK_SLOT
    const = lambda i: (0, 0)
    return pl.pallas_call(
        _mla_prep_kernel,
        grid=(nt,),
        in_specs=[
            pl.BlockSpec((TILE, d), lambda i: (i, 0)),
            pl.BlockSpec((None, d, LAT_W), lambda i: (layer, 0, 0)),
            pl.BlockSpec((1, Q_LORA), const),
            pl.BlockSpec((1, KV_LORA), const),
            pl.BlockSpec((hq, Q_LORA), const),
            pl.BlockSpec((KV_LORA, MLA_HEADS * NOPE_DIM), const),
            pl.BlockSpec((MLA_W, KV_LORA), const),
            pl.BlockSpec((TILE, 2 * ROPE_DIM), lambda i: (i % nblk, 0)),
            pl.BlockSpec((TILE, 2 * ROPE_DIM), lambda i: (i % nblk, 0)),
            pl.BlockSpec((ROPE_DIM, TILE), lambda i: (0, i % nblk)),
            pl.BlockSpec((ROPE_DIM, TILE), lambda i: (0, i % nblk)),
        ],
        out_specs=[
            pl.BlockSpec((None, hq, TILE), lambda i: (i, 0, 0)),
            pl.BlockSpec((TILE, hq), lambda i: (i, 0)),
            pl.BlockSpec((None, MLA_HEADS * MLA_V_SLOT, TILE), lambda i: (i, 0, 0)),
        ],
        out_shape=[
            jax.ShapeDtypeStruct((nt, hq, TILE), BF16),
            jax.ShapeDtypeStruct((m, hq), BF16),
            jax.ShapeDtypeStruct((nt, MLA_HEADS * MLA_V_SLOT, TILE), BF16),
        ],
        compiler_params=_cparams(1),
        name="mla_prep",
    )(hn, w_lat, gq.reshape(1, -1), gkv.reshape(1, -1), wqt, wkn, wvt, t1, t2, cct, sst)


def _mla_attn_kernel(qt_ref, k_ref, vt_ref, o_ref, acc_ref, sa_ref, sb_ref):
    qi = pl.program_id(2)

    def scores(j, s_ref):
        for g in range(HEAD_GROUP):
            qs = slice(g * MLA_QK_SLOT, (g + 1) * MLA_QK_SLOT)
            s_ref[g] = _dot(k_ref[j, :, qs], qt_ref[qs, :])

    def consume(j, s_ref, ms, masked):
        new = []
        for g in range(HEAD_GROUP):
            vs = slice(g * MLA_V_SLOT, (g + 1) * MLA_V_SLOT)
            s = s_ref[g]
            if masked:
                key = lax.broadcasted_iota(jnp.int32, (TILE, TILE), 0)
                qry = lax.broadcasted_iota(jnp.int32, (TILE, TILE), 1)
                s = jnp.where(key <= qry, s, -1e30)
            m_new = jnp.maximum(ms[g], jnp.max(s, axis=0, keepdims=True))
            alpha = jnp.exp2(ms[g] - m_new)
            p = jnp.exp2(s - m_new).astype(BF16)
            acc_ref[g] = alpha * acc_ref[g] + _dot(vt_ref[j, vs, :], p)
            new.append(m_new)
        return tuple(new)

    acc_ref[...] = jnp.zeros(acc_ref.shape, F32)
    scores(0, sa_ref)

    def pair(i, ms):
        j = 2 * i
        scores(j + 1, sb_ref)
        ms = consume(j, sa_ref, ms, False)
        scores(j + 2, sa_ref)
        return consume(j + 1, sb_ref, ms, False)

    init = tuple(jnp.full((1, TILE), -1e30, F32) for _ in range(HEAD_GROUP))
    ms = lax.fori_loop(0, qi // 2, pair, init)

    @pl.when(qi % 2 == 0)
    def _():
        consume(qi, sa_ref, ms, True)

    @pl.when(qi % 2 == 1)
    def _():
        scores(qi, sb_ref)
        consume(qi, sb_ref, consume(qi - 1, sa_ref, ms, False), True)

    for g in range(HEAD_GROUP):
        acc = acc_ref[g]
        o_ref[:, g * V_DIM:(g + 1) * V_DIM] = (acc[:V_DIM] / acc[V_DIM:V_DIM + 1]).T.astype(o_ref.dtype)


def _mla_attn(qt, k_cat, vt, batch, nblk):
    m = k_cat.shape[0]
    k3 = k_cat.reshape(batch * nblk, TILE, MLA_HEADS * MLA_QK_SLOT)
    g = HEAD_GROUP
    once = pl.Buffered(1)
    return pl.pallas_call(
        _mla_attn_kernel,
        grid=(batch, MLA_HEADS // g, nblk),
        in_specs=[
            pl.BlockSpec((None, g * MLA_QK_SLOT, TILE), lambda b, h, q: (b * nblk + q, h, 0)),
            pl.BlockSpec((nblk, TILE, g * MLA_QK_SLOT), lambda b, h, q: (b, 0, h), pipeline_mode=once),
            pl.BlockSpec((nblk, g * MLA_V_SLOT, TILE), lambda b, h, q: (b, h, 0), pipeline_mode=once),
        ],
        out_specs=pl.BlockSpec((TILE, g * V_DIM), lambda b, h, q: (b * nblk + q, h)),
        out_shape=jax.ShapeDtypeStruct((m, MLA_W), BF16),
        scratch_shapes=[pltpu.VMEM((g, MLA_V_SLOT, TILE), F32), pltpu.VMEM((g, TILE, TILE), F32),
                        pltpu.VMEM((g, TILE, TILE), F32)],
        compiler_params=_cparams(3),
        name="mla_attn",
    )(qt, k3, vt)


def _sb_attn_kernel(qt_ref, k_ref, vt_ref, o_ref, acc_ref, z_ref):
    qi = pl.program_id(2)
    scale = SB_HEAD_DIM ** -0.5
    key = lax.broadcasted_iota(jnp.int32, (TILE, TILE), 0)
    col = lax.broadcasted_iota(jnp.int32, (TILE, TILE), 1)
    later_keys = jnp.where(col > key, 1.0, 0.0).astype(BF16)

    def tile(j, carry, masked):
        heads = [slice(g * SB_HEAD_DIM, (g + 1) * SB_HEAD_DIM) for g in range(HEAD_GROUP)]
        valid = key < col
        for g, hs in enumerate(heads):
            z_ref[g] = _dot(k_ref[j, :, hs], qt_ref[hs, :])
        new = []
        for g in range(HEAD_GROUP):
            z = z_ref[g] * scale
            soft = jnp.log(1.0 + jnp.exp(-jnp.abs(z)))
            log_beta = jnp.minimum(z, 0.0) - soft
            log_1m_beta = log_beta - z
            if masked:
                log_1m_beta = jnp.where(valid, log_1m_beta, 0.0)
            hi = log_1m_beta.astype(BF16)
            lo = (log_1m_beta - hi.astype(F32)).astype(BF16)
            later = _dot(later_keys, hi) + _dot(later_keys, lo)
            z_ref[g] = log_beta + later
            new.append(carry[g] + later[0:1] + log_1m_beta[0:1])
        for g, hs in enumerate(heads):
            w = jnp.exp(z_ref[g] + carry[g])
            if masked:
                w = jnp.where(valid, w, 0.0)
            acc_ref[g] += _dot(vt_ref[j, hs, :], w.astype(BF16))
        return tuple(new)

    acc_ref[...] = jnp.zeros(acc_ref.shape, F32)
    carry = tile(qi, tuple(jnp.zeros((1, TILE), F32) for _ in range(HEAD_GROUP)), True)

    def cond(state):
        j, carry = state
        alive = functools.reduce(jnp.maximum, carry)
        return jnp.logical_and(j >= 0, jnp.max(alive) > SB_DEAD_LOG)

    def body(state):
        j, carry = state
        return j - 1, tile(j, carry, False)

    lax.while_loop(cond, body, (qi - 1, carry))
    for g in range(HEAD_GROUP):
        o_ref[:, g * SB_HEAD_DIM:(g + 1) * SB_HEAD_DIM] = acc_ref[g].T.astype(o_ref.dtype)


def _sb_attn(qt, k, vt, batch, nblk):
    m = k.shape[0]
    k3 = k.reshape(batch * nblk, TILE, SB_W)
    g = HEAD_GROUP
    return pl.pallas_call(
        _sb_attn_kernel,
        grid=(batch, SB_HEADS // g, nblk),
        in_specs=[
            pl.BlockSpec((None, g * SB_HEAD_DIM, TILE), lambda b, h, q: (b * nblk + q, h, 0)),
            pl.BlockSpec((nblk, TILE, g * SB_HEAD_DIM), lambda b, h, q: (b, 0, h)),
            pl.BlockSpec((nblk, g * SB_HEAD_DIM, TILE), lambda b, h, q: (b, h, 0)),
        ],
        out_specs=pl.BlockSpec((TILE, g * SB_HEAD_DIM), lambda b, h, q: (b * nblk + q, h)),
        out_shape=jax.ShapeDtypeStruct((m, SB_W), BF16),
        scratch_shapes=[pltpu.VMEM((g, SB_HEAD_DIM, TILE), F32), pltpu.VMEM((g, TILE, TILE), F32)],
        compiler_params=_cparams(3),
        name="sb_attn",
    )(qt, k3, vt)


def _merge_kernel(osb_ref, omla_ref, gate_ref, h_ref, wsb_ref, wmla_ref, wo_ref, g_ref, h_out_ref, hn_out_ref):
    a = _dot(osb_ref[...], wsb_ref[...])
    b = _dot(omla_ref[...], wmla_ref[...])
    merged = gate_ref[:, :D_MODEL].astype(F32) * a + gate_ref[:, D_MODEL:].astype(F32) * b
    y = h_ref[...] + _dot(merged.astype(BF16), wo_ref[...])
    h_out_ref[...] = y
    hn_out_ref[...] = _rms(y, g_ref[...]).astype(hn_out_ref.dtype)


def _merge(o_sb, o_mla, gates, h, wsb, wmla, wo, layer, g, tm):
    m = h.shape[0]
    row = lambda i: (i, 0)
    const = lambda i: (0, 0)
    this_layer = lambda i: (layer, 0, 0)
    once = pl.Buffered(1)
    return pl.pallas_call(
        _merge_kernel,
        grid=(m // tm,),
        in_specs=[
            pl.BlockSpec((tm, SB_W), row),
            pl.BlockSpec((tm, MLA_W), row),
            pl.BlockSpec((tm, 2 * D_MODEL), row),
            pl.BlockSpec((tm, D_MODEL), row),
            pl.BlockSpec((None, SB_W, D_MODEL), this_layer, pipeline_mode=once),
            pl.BlockSpec((None, MLA_W, D_MODEL), this_layer, pipeline_mode=once),
            pl.BlockSpec((None, D_MODEL, D_MODEL), this_layer, pipeline_mode=once),
            pl.BlockSpec((1, D_MODEL), const),
        ],
        out_specs=[pl.BlockSpec((tm, D_MODEL), row), pl.BlockSpec((tm, D_MODEL), row)],
        out_shape=[jax.ShapeDtypeStruct((m, D_MODEL), F32), jax.ShapeDtypeStruct((m, D_MODEL), BF16)],
        compiler_params=_cparams(1),
        name="merge_out_proj",
    )(o_sb, o_mla, gates, h, wsb, wmla, wo, g.reshape(1, -1))


def _ffn_up_kernel(hn_ref, wa_ref, wg_ref, cwa_ref, cwg_ref, cba_ref, cbg_ref, o_ref, wa_bf, wg_bf, bufa, bufg, *,
                   tiles_per_seq):
    i = pl.program_id(1)
    tm, tf = o_ref.shape

    @pl.when(i == 0)
    def _():
        wa_bf[...] = wa_ref[...].astype(BF16)
        wg_bf[...] = wg_ref[...].astype(BF16)

    @pl.when(i % tiles_per_seq == 0)
    def _():
        bufa[0:8, :] = jnp.zeros((8, tf), F32)
        bufg[0:8, :] = jnp.zeros((8, tf), F32)

    def conv(w_bf, cw_ref, cb_ref, buf):
        buf[8:8 + tm, :] = _dot(hn_ref[...], w_bf[...])
        u = buf[...]
        back1 = pltpu.roll(u, 1, axis=0)[8:]
        back2 = pltpu.roll(u, 2, axis=0)[8:]
        out = cb_ref[...] + cw_ref[0:1, :] * back2 + cw_ref[1:2, :] * back1 + cw_ref[2:3, :] * u[8:]
        buf[0:8, :] = buf[tm:tm + 8, :]
        return out

    a = conv(wa_bf, cwa_ref, cba_ref, bufa)
    g = conv(wg_bf, cwg_ref, cbg_ref, bufg)
    o_ref[...] = (a * jax.nn.sigmoid(a) * g).astype(o_ref.dtype)


def _ffn_up(hn, w_up, conv_w, conv_b, layer, tm, tf, seq_rows):
    m, d = hn.shape
    nf = D_FF // tf
    conv_b3 = conv_b.reshape(conv_b.shape[0], 1, 2 * D_FF)
    a_cols = lambda j, i: (layer, 0, j)
    g_cols = lambda j, i: (layer, 0, nf + j)
    buf = pltpu.VMEM((tm + 8, tf), F32)
    return pl.pallas_call(
        functools.partial(_ffn_up_kernel, tiles_per_seq=seq_rows // tm),
        grid=(nf, m // tm),
        in_specs=[
            pl.BlockSpec((tm, d), lambda j, i: (i, 0)),
            pl.BlockSpec((None, d, tf), a_cols),
            pl.BlockSpec((None, d, tf), g_cols),
            pl.BlockSpec((None, CONV_W, tf), a_cols),
            pl.BlockSpec((None, CONV_W, tf), g_cols),
            pl.BlockSpec((None, 1, tf), a_cols),
            pl.BlockSpec((None, 1, tf), g_cols),
        ],
        out_specs=pl.BlockSpec((tm, tf), lambda j, i: (i, j)),
        out_shape=jax.ShapeDtypeStruct((m, D_FF), BF16),
        scratch_shapes=[pltpu.VMEM((d, tf), BF16), pltpu.VMEM((d, tf), BF16), buf, buf],
        compiler_params=_cparams(2),
        name="ffn_up_conv_glu",
    )(hn, w_up, w_up, conv_w, conv_w, conv_b3, conv_b3)


def _ffn_down_kernel(act_ref, h_ref, w_ref, g_ref, *out_refs, final):
    y = h_ref[...] + _dot(act_ref[...], w_ref[...])
    if final:
        out_refs[0][...] = _rms(y, g_ref[...])
    else:
        out_refs[0][...] = y
        out_refs[1][...] = _rms(y, g_ref[...]).astype(out_refs[1].dtype)


def _ffn_down(act, h, w, layer, g, tm, final=None):
    m = h.shape[0]
    const = lambda i: (0, 0)
    out_spec = pl.BlockSpec((tm, D_MODEL), lambda i: (i, 0))
    if final:
        n_seq, seq_rows, first, kept = final
        assert kept % tm == 0 and first % 8 == 0
        per_seq = kept // tm
        n_tiles = n_seq * per_seq
        start = lambda i: (pl.multiple_of((i // per_seq) * seq_rows + first + (i % per_seq) * tm, 8), 0)
        act_spec = pl.BlockSpec((pl.Element(tm), pl.Element(D_FF)), start)
        h_spec = pl.BlockSpec((pl.Element(tm), pl.Element(D_MODEL)), start)
        out_specs, out_shape = [out_spec], [jax.ShapeDtypeStruct((n_seq * kept, D_MODEL), F32)]
    else:
        n_tiles = m // tm
        act_spec = pl.BlockSpec((tm, D_FF), lambda i: (i, 0))
        h_spec = pl.BlockSpec((tm, D_MODEL), lambda i: (i, 0))
        out_specs = [out_spec, out_spec]
        out_shape = [jax.ShapeDtypeStruct((m, D_MODEL), F32), jax.ShapeDtypeStruct((m, D_MODEL), BF16)]
    return pl.pallas_call(
        functools.partial(_ffn_down_kernel, final=bool(final)),
        grid=(n_tiles,),
        in_specs=[
            act_spec,
            h_spec,
            pl.BlockSpec((None, D_FF, D_MODEL), lambda i: (layer, 0, 0), pipeline_mode=pl.Buffered(1)),
            pl.BlockSpec((1, D_MODEL), const),
        ],
        out_specs=out_specs,
        out_shape=out_shape,
        compiler_params=_cparams(1),
        name="ffn_down",
    )(act, h, w, g.reshape(1, -1))


def _rotate_partner(w):
    half = w.shape[-1] // 2
    return jnp.concatenate([-w[..., half:], w[..., :half]], axis=-1)


def _rope_tables(lp):
    half = ROPE_DIM // 2
    freqs = ROPE_THETA ** (-jnp.arange(half, dtype=F32) / half)
    ang = jnp.arange(lp).astype(F32)[:, None] * freqs[None, :]
    cc = jnp.concatenate([jnp.cos(ang)] * 2, axis=1)
    ss = jnp.concatenate([jnp.sin(ang)] * 2, axis=1)
    pad = jnp.zeros_like(cc)
    return jnp.concatenate([cc, pad], axis=1), jnp.concatenate([ss, pad], axis=1), cc.T, ss.T


def kernel(x, meta_tokens, norm_mix, w_in, q_norm, w_uq, kv_norm, w_ukv, w_sb_out, w_mla_out, w_o, norm_ffn,
           w_up, conv_w, conv_b, w_down, final_norm):
    batch, seq, d = x.shape
    depth = w_in.shape[0]
    l = N_META + seq
    nblk = -(-l // TILE)
    lp = nblk * TILE
    m = batch * lp
    tm = 3 * TILE
    assert d == D_MODEL and m % tm == 0 and lp % tm == 0 and seq % TILE == 0 and meta_tokens.shape[0] == N_META

    t1, t2, cct, sst = _rope_tables(lp)

    o3 = 3 * SB_W
    o4 = o3 + Q_LORA + KV_LORA
    o5 = o4 + ROPE_DIM
    w_lat = jnp.concatenate([w_in[:, :, o3:o5], _rotate_partner(w_in[:, :, o4:o5])], axis=2).astype(BF16)
    w_gate = w_in[:, :, o5:].astype(BF16)
    w_qkv = w_in[:, :, :o3].astype(BF16)
    w_sb_out, w_mla_out, w_o, w_down = (w.astype(BF16) for w in (w_sb_out, w_mla_out, w_o, w_down))

    meta_tile = jnp.pad(meta_tokens.astype(F32), ((0, TILE - N_META), (0, 0)))
    h, hn = _embed(x.reshape(batch * seq, d), meta_tile, norm_mix[0], batch, seq, nblk)
    out = None
    for i in range(depth):
        uq = w_uq[i].reshape(Q_LORA, MLA_HEADS, NOPE_DIM + ROPE_DIM)
        uq_rope = uq[..., NOPE_DIM:]
        wq_ext_t = jnp.concatenate([uq[..., :NOPE_DIM], uq_rope, _rotate_partner(uq_rope)], axis=-1) \
            .reshape(Q_LORA, MLA_HEADS * MLA_QK_SLOT).T.astype(BF16)
        ukv = w_ukv[i].reshape(KV_LORA, MLA_HEADS, NOPE_DIM + V_DIM)
        wkn = ukv[..., :NOPE_DIM].reshape(KV_LORA, MLA_HEADS * NOPE_DIM).astype(BF16)
        wv_mla_t = ukv[..., NOPE_DIM:].reshape(KV_LORA, MLA_W).T.astype(BF16)

        q_sb_t = _proj(hn, w_qkv, i, 0, SB_W, BF16, tm, SB_W, transposed=True, name="proj_q_sb")
        k_sb = _proj(hn, w_qkv, i, SB_W, SB_W, BF16, tm, SB_W, name="proj_k_sb")
        v_sb_t = _proj(hn, w_qkv, i, 2 * SB_W, SB_W, BF16, tm, SB_W, transposed=True, name="proj_v_sb")
        gates = _proj(hn, w_gate, i, 0, 2 * D_MODEL, BF16, tm, 1024, act="sigmoid", name="proj_gates")

        o_sb = _sb_attn(q_sb_t, k_sb, v_sb_t, batch, nblk)
        q_t, k_cat, v_t = _mla_prep(hn, w_lat, i, q_norm[i], kv_norm[i], wq_ext_t, wkn, wv_mla_t, t1, t2, cct, sst,
                                    nblk)
        o_mla = _mla_attn(q_t, k_cat, v_t, batch, nblk)

        h, hn = _merge(o_sb, o_mla, gates, h, w_sb_out, w_mla_out, w_o, i, norm_ffn[i], TILE)

        act = _ffn_up(hn, w_up, conv_w, conv_b, i, tm, 512, lp)
        if i + 1 < depth:
            h, hn = _ffn_down(act, h, w_down, i, norm_mix[i + 1], TILE)
        else:
            (out,) = _ffn_down(act, h, w_down, i, final_norm, TILE, final=(batch, lp, N_META, seq))

    return out.reshape(batch, seq, d)
```
